```python
import math
import jax
import jax.numpy as jnp
from jax import lax
import numpy as np

D_MODEL = 1024
BATCH = 2
SEQ = 8192
DEPTH = 4
DEC_BATCH = 128
DEC_SEQ = 1
PAST_LEN = 2048
PAGE_SIZE = 128

HEAD_DIM = 64
SB_HEADS = 8
SB_KV = 4
SB_REP = SB_HEADS // SB_KV
NSA_HEADS = 8
NSA_KV = 2
NSA_REP = NSA_HEADS // NSA_KV
NSA_BLOCK = 64
NSA_SEL = 16
NSA_WINDOW = 512
DIFF_HEADS = 4
DIFF_KV = 2
DIFF_REP = DIFF_HEADS // DIFF_KV
DIFF_VDIM = 2 * HEAD_DIM
ROPE_THETA = 500000.0
ROT_DIM = HEAD_DIM // 4
D_FF = 2816
CONV_W = 3
Q_BLOCK = 128
EPS = 1e-6
NEG_INF = -1e30

SB_W = SB_HEADS * HEAD_DIM
SB_KVW = SB_KV * HEAD_DIM
NSA_W = NSA_HEADS * HEAD_DIM
NSA_KVW = NSA_KV * HEAD_DIM
DIFF_QW = DIFF_HEADS * 2 * HEAD_DIM
DIFF_KW = DIFF_KV * 2 * HEAD_DIM
DIFF_VW = DIFF_KV * DIFF_VDIM
DIFF_OW = DIFF_HEADS * DIFF_VDIM

IN_SPLIT = (
    ('sb_q', SB_W), ('sb_k', SB_KVW), ('sb_v', SB_KVW),
    ('nsa_q', NSA_W), ('cmp_k', NSA_KVW), ('cmp_v', NSA_KVW), ('sel_k', NSA_KVW),
    ('sel_v', NSA_KVW), ('win_k', NSA_KVW), ('win_v', NSA_KVW), ('nsa_g', 3 * NSA_HEADS),
    ('diff_q', DIFF_QW), ('diff_k', DIFF_KW), ('diff_v', DIFF_VW),
    ('merge_g', 3 * D_MODEL),
)
N_IN = sum(w for _, w in IN_SPLIT)

kernel_name = 'hybrid_stickbreak_nsa_diffattn_convffn_step'


def rms_norm(x, g):
    xf = x.astype(jnp.float32)
    y = xf * lax.rsqrt(jnp.mean(xf * xf, axis=-1, keepdims=True) + EPS)
    return (y * g.astype(jnp.float32)).astype(x.dtype)


def masked_softmax(s, mask, axis=-1):
    p = jax.nn.softmax(jnp.where(mask, s, NEG_INF), axis=axis)
    return jnp.where(mask, p, 0.0)


def rope_partial(x, pos):
    half = ROT_DIM // 2
    inv_freq = ROPE_THETA ** (-jnp.arange(half, dtype=jnp.float32) * 2.0 / ROT_DIM)
    ang = pos.astype(jnp.float32)[:, None] * inv_freq
    ang = ang.reshape((1, pos.shape[0]) + (1,) * (x.ndim - 3) + (half,))
    cos, sin = jnp.cos(ang), jnp.sin(ang)
    xf = x.astype(jnp.float32)
    x1, x2 = xf[..., :half], xf[..., half:ROT_DIM]
    out = jnp.concatenate([x1 * cos - x2 * sin, x2 * cos + x1 * sin, xf[..., ROT_DIM:]], axis=-1)
    return out.astype(x.dtype)


def query_sweep(fn, q_arrays, q_pos):
    b, sq = q_arrays[0].shape[:2]
    qb = min(Q_BLOCK, sq)
    nb = -(-sq // qb)
    pad = nb * qb - sq

    def split(a):
        a = jnp.pad(a, ((0, 0), (0, pad)) + ((0, 0),) * (a.ndim - 2))
        return jnp.swapaxes(a.reshape((b, nb, qb) + a.shape[2:]), 0, 1)

    pos = jnp.pad(q_pos, (0, pad), mode='edge').reshape(nb, qb)
    out = lax.map(lambda args: fn(*args), tuple(split(a) for a in q_arrays) + (pos,))
    out = jnp.swapaxes(out, 0, 1)
    return out.reshape((b, nb * qb) + out.shape[3:])[:, :sq]


def stick_breaking_attention(q, k, v, q_pos):
    b, lk = k.shape[:2]
    kpos = jnp.arange(lk)
    scale = HEAD_DIM ** -0.5

    def block(qb, pos):
        nq = qb.shape[1]
        qg = qb.reshape(b, nq, SB_KV, SB_REP, HEAD_DIM)
        z = jnp.einsum('bqgrd,bkgd->bgrqk', qg, k).astype(jnp.float32) * scale
        mask = kpos[None, :] < pos[:, None]
        log_keep = jnp.where(mask, jax.nn.log_sigmoid(-z), 0.0)
        between = lax.cumsum(log_keep, axis=4, reverse=True) - log_keep
        a = jnp.where(mask, jnp.exp(jax.nn.log_sigmoid(z) + between), 0.0)
        o = jnp.einsum('bgrqk,bkgd->bqgrd', a.astype(v.dtype), v)
        return o.reshape(b, nq, SB_HEADS, HEAD_DIM)

    return query_sweep(block, (q,), q_pos)


def nsa_attention(q, gates, kc, vc, ks, vs, kw, vw, q_pos, win_start, cmp_pe, cmp_wk, cmp_wv):
    b, lk = kc.shape[:2]
    nblk = -(-lk // NSA_BLOCK)
    padk = nblk * NSA_BLOCK - lk

    def to_blocks(a):
        a = jnp.pad(a, ((0, 0), (0, padk), (0, 0), (0, 0)))
        return a.reshape(b, nblk, NSA_BLOCK, NSA_KV, HEAD_DIM)

    kcb = jnp.einsum('bnjgd,jde->bnge', to_blocks(kc) + cmp_pe[:, None, :], cmp_wk)
    vcb = jnp.einsum('bnjgd,jde->bnge', to_blocks(vc) + cmp_pe[:, None, :], cmp_wv)
    ksb = jnp.moveaxis(to_blocks(ks), 3, 1)
    vsb = jnp.moveaxis(to_blocks(vs), 3, 1)
    blk_ids = jnp.arange(nblk)
    blk_end = (blk_ids + 1) * NSA_BLOCK - 1
    n_top = min(NSA_SEL - 1, nblk)
    lw = kw.shape[1]
    wpad = ((0, 0), (NSA_WINDOW, Q_BLOCK), (0, 0), (0, 0))
    kwp = jnp.pad(kw, wpad)
    vwp = jnp.pad(vw, wpad)
    wpos = win_start - NSA_WINDOW + jnp.arange(lw + NSA_WINDOW + Q_BLOCK)
    win_real = (wpos >= win_start) & (wpos < win_start + lw)
    scale = HEAD_DIM ** -0.5
    gather = jax.vmap(jax.vmap(lambda blocks, idx: blocks[idx]))

    def block(qb, gb, pos):
        nq = qb.shape[1]
        qg = qb.reshape(b, nq, NSA_KV, NSA_REP, HEAD_DIM)
        sc = jnp.einsum('bqgrd,bngd->bgrqn', qg, kcb).astype(jnp.float32) * scale
        pc = masked_softmax(sc, blk_end[None, :] <= pos[:, None])
        o_cmp = jnp.einsum('bgrqn,bngd->bqgrd', pc.astype(vcb.dtype), vcb)
        cur = pos // NSA_BLOCK
        imp = jnp.where(blk_ids[None, :] < cur[:, None], pc.sum(axis=2), -1.0)
        top_val, top_idx = lax.top_k(imp, n_top)
        idx = jnp.concatenate([jnp.broadcast_to(cur[None, None, :, None], (b, NSA_KV, nq, 1)), top_idx], axis=-1)
        ok = jnp.concatenate([jnp.ones((b, NSA_KV, nq, 1), bool), top_val >= 0.0], axis=-1)
        kg = gather(ksb, idx)
        vg = gather(vsb, idx)
        kpos = idx[..., None] * NSA_BLOCK + jnp.arange(NSA_BLOCK)
        smask = ok[..., None] & (kpos <= pos[None, None, :, None, None])
        ss = jnp.einsum('bqgrd,bgqnjd->bgrqnj', qg, kg).astype(jnp.float32) * scale
        ps = masked_softmax(ss, smask[:, :, None], axis=(-2, -1))
        o_sel = jnp.einsum('bgrqnj,bgqnjd->bqgrd', ps.astype(vg.dtype), vg)
        i0 = pos[0] - win_start
        kwin = lax.dynamic_slice_in_dim(kwp, i0, NSA_WINDOW + nq, axis=1)
        vwin = lax.dynamic_slice_in_dim(vwp, i0, NSA_WINDOW + nq, axis=1)
        pw = lax.dynamic_slice_in_dim(wpos, i0, NSA_WINDOW + nq)
        rw = lax.dynamic_slice_in_dim(win_real, i0, NSA_WINDOW + nq)
        wmask = rw[None, :] & (pw[None, :] <= pos[:, None]) & (pw[None, :] >= pos[:, None] - NSA_WINDOW)
        sw = jnp.einsum('bqgrd,bkgd->bgrqk', qg, kwin).astype(jnp.float32) * scale
        pwin = masked_softmax(sw, wmask)
        o_win = jnp.einsum('bgrqk,bkgd->bqgrd', pwin.astype(vwin.dtype), vwin)
        gg = gb.reshape(b, nq, NSA_KV, NSA_REP, 3, 1)
        o = gg[:, :, :, :, 0] * o_cmp + gg[:, :, :, :, 1] * o_sel + gg[:, :, :, :, 2] * o_win
        return o.reshape(b, nq, NSA_HEADS, HEAD_DIM)

    return query_sweep(block, (q, gates), q_pos)


def diff_attention(q, k, v, q_pos, lam, lam_init, subln_g):
    b, lk = k.shape[:2]
    kpos = jnp.arange(lk)
    scale = HEAD_DIM ** -0.5

    def block(qb, pos):
        nq = qb.shape[1]
        qg = qb.reshape(b, nq, DIFF_KV, DIFF_REP, 2, HEAD_DIM)
        s = jnp.einsum('bqgrmd,bkgmd->bgrmqk', qg, k).astype(jnp.float32) * scale
        p = masked_softmax(s, kpos[None, :] <= pos[:, None])
        a = p[:, :, :, 0] - lam * p[:, :, :, 1]
        o = jnp.einsum('bgrqk,bkgv->bqgrv', a.astype(v.dtype), v)
        return o.reshape(b, nq, DIFF_HEADS, DIFF_VDIM)

    o = query_sweep(block, (q,), q_pos)
    return rms_norm(o, subln_g) * (1.0 - lam_init)


def project(h, w_in, pos):
    b, s, _ = h.shape
    u = h @ w_in
    parts = {}
    off = 0
    for name, width in IN_SPLIT:
        parts[name] = u[..., off:off + width]
        off += width
    hv = lambda a, n, d: a.reshape(b, s, n, d)
    proj = {
        'sb_q': hv(parts['sb_q'], SB_HEADS, HEAD_DIM),
        'nsa_q': rope_partial(hv(parts['nsa_q'], NSA_HEADS, HEAD_DIM), pos),
        'nsa_g': jax.nn.sigmoid(hv(parts['nsa_g'], NSA_HEADS, 3)),
        'diff_q': rope_partial(parts['diff_q'].reshape(b, s, DIFF_HEADS, 2, HEAD_DIM), pos),
        'merge_g': jax.nn.sigmoid(parts['merge_g'].reshape(b, s, 3, D_MODEL)),
    }
    sb_rows = jnp.stack([hv(parts['sb_k'], SB_KV, HEAD_DIM), hv(parts['sb_v'], SB_KV, HEAD_DIM)], axis=2)
    nsa_rows = jnp.stack([
        rope_partial(hv(parts['cmp_k'], NSA_KV, HEAD_DIM), pos), hv(parts['cmp_v'], NSA_KV, HEAD_DIM),
        rope_partial(hv(parts['sel_k'], NSA_KV, HEAD_DIM), pos), hv(parts['sel_v'], NSA_KV, HEAD_DIM)], axis=2)
    win_rows = jnp.stack([rope_partial(hv(parts['win_k'], NSA_KV, HEAD_DIM), pos),
                          hv(parts['win_v'], NSA_KV, HEAD_DIM)], axis=2)
    dk = rope_partial(parts['diff_k'].reshape(b, s, DIFF_KV, 2, HEAD_DIM), pos).reshape(b, s, DIFF_KV, DIFF_VDIM)
    diff_rows = jnp.stack([dk, hv(parts['diff_v'], DIFF_KV, DIFF_VDIM)], axis=2)
    return proj, sb_rows, nsa_rows, win_rows, diff_rows


def token_mix(proj, pos, sb_kv, nsa_kv, win_kv, win_start, diff_kv, cmp_pe, cmp_wk, cmp_wv,
              lam_vec, lam_init, subln_g, w_sb_out, w_nsa_out, w_diff_out, w_o):
    b, s = proj['sb_q'].shape[:2]
    o_sb = stick_breaking_attention(proj['sb_q'], sb_kv[:, :, 0], sb_kv[:, :, 1], pos)
    o_nsa = nsa_attention(proj['nsa_q'], proj['nsa_g'], nsa_kv[:, :, 0], nsa_kv[:, :, 1], nsa_kv[:, :, 2],
                          nsa_kv[:, :, 3], win_kv[:, :, 0], win_kv[:, :, 1], pos, win_start, cmp_pe, cmp_wk, cmp_wv)
    lv = lam_vec.astype(jnp.float32)
    lam = jnp.exp(jnp.sum(lv[0] * lv[1])) - jnp.exp(jnp.sum(lv[2] * lv[3])) + lam_init
    lk = diff_kv.shape[1]
    o_diff = diff_attention(proj['diff_q'], diff_kv[:, :, 0].reshape(b, lk, DIFF_KV, 2, HEAD_DIM),
                            diff_kv[:, :, 1], pos, lam, lam_init, subln_g)
    g = proj['merge_g']
    merged = (g[:, :, 0] * (o_sb.reshape(b, s, SB_W) @ w_sb_out)
              + g[:, :, 1] * (o_nsa.reshape(b, s, NSA_W) @ w_nsa_out)
              + g[:, :, 2] * (o_diff.reshape(b, s, DIFF_OW) @ w_diff_out))
    return merged @ w_o


def conv_ffn(h, prev, w_up, conv_w, conv_b, w_down):
    s = h.shape[1]
    u = h @ w_up
    a, gate = u[..., :D_FF], u[..., D_FF:]
    ap = jnp.concatenate([prev.astype(a.dtype), a], axis=1)
    c = conv_b + ap[:, 0:s] * conv_w[0]
    for j in range(1, CONV_W):
        c = c + ap[:, j:j + s] * conv_w[j]
    y = (jax.nn.gelu(c) * gate) @ w_down
    return y, ap[:, ap.shape[1] - (CONV_W - 1):]


def setup_inputs(seed: int = 0) -> dict:
    key = jax.random.key(seed)
    ks = iter(jax.random.split(key, 32))
    nrm = lambda shape, scale=1.0: jax.random.normal(next(ks), shape, jnp.float32) * scale
    n_pages = PAST_LEN // PAGE_SIZE
    n_use = DEC_BATCH * n_pages
    n_pool = n_use + max(1, n_use // 4)
    win_keep = min(NSA_WINDOW, PAST_LEN)
    page_table = jax.random.permutation(next(ks), n_pool)[:n_use].reshape(DEC_BATCH, n_pages).astype(jnp.int32)
    return {
        'x_prompt': nrm((BATCH, SEQ, D_MODEL)),
        'x_sample': nrm((DEC_BATCH, DEC_SEQ, D_MODEL)),
        'cache_sb_kv': nrm((DEPTH, n_pool, PAGE_SIZE, 2, SB_KV, HEAD_DIM)),
        'cache_nsa_kv': nrm((DEPTH, n_pool, PAGE_SIZE, 4, NSA_KV, HEAD_DIM)),
        'cache_diff_kv': nrm((DEPTH, n_pool, PAGE_SIZE, 2, DIFF_KV, DIFF_VDIM)),
        'state_nsa_win': nrm((DEPTH, DEC_BATCH, win_keep, 2, NSA_KV, HEAD_DIM)),
        'state_ffn_conv': nrm((DEPTH, DEC_BATCH, CONV_W - 1, D_FF)),
        'page_table': page_table,
        'ln_attn': 1.0 + nrm((DEPTH, D_MODEL), 0.05),
        'w_in': nrm((DEPTH, D_MODEL, N_IN), D_MODEL ** -0.5),
        'nsa_cmp_pe': nrm((DEPTH, NSA_BLOCK, HEAD_DIM), 0.1),
        'nsa_cmp_wk': nrm((DEPTH, NSA_BLOCK, HEAD_DIM, HEAD_DIM), (NSA_BLOCK * HEAD_DIM) ** -0.5),
        'nsa_cmp_wv': nrm((DEPTH, NSA_BLOCK, HEAD_DIM, HEAD_DIM), (NSA_BLOCK * HEAD_DIM) ** -0.5),
        'diff_lambda': nrm((DEPTH, 4, HEAD_DIM), 0.1),
        'diff_subln': 1.0 + nrm((DEPTH, DIFF_VDIM), 0.05),
        'w_sb_out': nrm((DEPTH, SB_W, D_MODEL), SB_W ** -0.5),
        'w_nsa_out': nrm((DEPTH, NSA_W, D_MODEL), NSA_W ** -0.5),
        'w_diff_out': nrm((DEPTH, DIFF_OW, D_MODEL), DIFF_OW ** -0.5),
        'w_o': nrm((DEPTH, D_MODEL, D_MODEL), D_MODEL ** -0.5),
        'ln_ffn': 1.0 + nrm((DEPTH, D_MODEL), 0.05),
        'w_ff_up': nrm((DEPTH, D_MODEL, 2 * D_FF), D_MODEL ** -0.5),
        'ffn_conv_w': nrm((DEPTH, CONV_W, D_FF), 0.5),
        'ffn_conv_b': nrm((DEPTH, D_FF), 0.02),
        'w_ff_down': nrm((DEPTH, D_FF, D_MODEL), D_FF ** -0.5),
        'ln_final': 1.0 + nrm((D_MODEL,), 0.05),
    }


def reference(x_prompt, x_sample, cache_sb_kv, cache_nsa_kv, cache_diff_kv, state_nsa_win, state_ffn_conv,
              page_table, ln_attn, w_in, nsa_cmp_pe, nsa_cmp_wk, nsa_cmp_wv, diff_lambda, diff_subln,
              w_sb_out, w_nsa_out, w_diff_out, w_o, ln_ffn, w_ff_up, ffn_conv_w, ffn_conv_b, w_ff_down, ln_final):
    pos_p = jnp.arange(SEQ, dtype=jnp.int32)
    pos_s = PAST_LEN + jnp.arange(DEC_SEQ, dtype=jnp.int32)
    win_keep_p = min(NSA_WINDOW, SEQ)
    win_keep_s = min(NSA_WINDOW, PAST_LEN)
    win_start_s = PAST_LEN - win_keep_s
    past_len = page_table.shape[1] * PAGE_SIZE

    def gather_past(cache, l):
        rows = cache[l, page_table]
        return rows.reshape((DEC_BATCH, past_len) + cache.shape[3:])

    xp, xs = x_prompt, x_sample
    sb_p, sb_s, nsa_p, nsa_s, diff_p, diff_s = [], [], [], [], [], []
    win_p, win_s, conv_p, conv_s = [], [], [], []
    for l in range(DEPTH):
        lam_init = 0.8 - 0.6 * math.exp(-0.3 * l)
        mix_w = (nsa_cmp_pe[l], nsa_cmp_wk[l], nsa_cmp_wv[l], diff_lambda[l], lam_init, diff_subln[l],
                 w_sb_out[l], w_nsa_out[l], w_diff_out[l], w_o[l])
        proj, sb_r, nsa_r, win_r, diff_r = project(rms_norm(xp, ln_attn[l]), w_in[l], pos_p)
        xp = xp + token_mix(proj, pos_p, sb_r, nsa_r, win_r, 0, diff_r, *mix_w)
        f, conv_new = conv_ffn(rms_norm(xp, ln_ffn[l]), jnp.zeros((BATCH, CONV_W - 1, D_FF), xp.dtype),
                               w_ff_up[l], ffn_conv_w[l], ffn_conv_b[l], w_ff_down[l])
        xp = xp + f
        sb_p.append(sb_r)
        nsa_p.append(nsa_r)
        diff_p.append(diff_r)
        win_p.append(win_r[:, SEQ - win_keep_p:])
        conv_p.append(conv_new)
        proj, sb_r, nsa_r, win_r, diff_r = project(rms_norm(xs, ln_attn[l]), w_in[l], pos_s)
        sb_full = jnp.concatenate([gather_past(cache_sb_kv, l), sb_r], axis=1)
        nsa_full = jnp.concatenate([gather_past(cache_nsa_kv, l), nsa_r], axis=1)
        diff_full = jnp.concatenate([gather_past(cache_diff_kv, l), diff_r], axis=1)
        win_full = jnp.concatenate([state_nsa_win[l], win_r], axis=1)
        xs = xs + token_mix(proj, pos_s, sb_full, nsa_full, win_full, win_start_s, diff_full, *mix_w)
        f, conv_new = conv_ffn(rms_norm(xs, ln_ffn[l]), state_ffn_conv[l],
                               w_ff_up[l], ffn_conv_w[l], ffn_conv_b[l], w_ff_down[l])
        xs = xs + f
        sb_s.append(sb_r)
        nsa_s.append(nsa_r)
        diff_s.append(diff_r)
        win_s.append(win_full[:, win_full.shape[1] - win_keep_s:])
        conv_s.append(conv_new)
    return (rms_norm(xp, ln_final), rms_norm(xs, ln_final),
            jnp.stack(sb_p), jnp.stack(sb_s), jnp.stack(nsa_p), jnp.stack(nsa_s),
            jnp.stack(diff_p), jnp.stack(diff_s), jnp.stack(win_p), jnp.stack(win_s),
            jnp.stack(conv_p), jnp.stack(conv_s))
```

```python
import functools
import math

import jax
import jax.numpy as jnp
from jax import lax
from jax.experimental import pallas as pl
from jax.experimental.pallas import tpu as pltpu

F32 = jnp.float32
BF16 = jnp.bfloat16

HEAD_DIM = 64
SB_HEADS, SB_KV = 8, 4
NSA_HEADS, NSA_KV = 8, 2
NSA_REP = NSA_HEADS // NSA_KV
NSA_BLOCK = 64
NSA_SEL = 16
NSA_WINDOW = 512
DIFF_HEADS, DIFF_KV = 4, 2
DIFF_VDIM = 2 * HEAD_DIM
ROPE_THETA = 500000.0
ROT_DIM = HEAD_DIM // 4
CONV_W = 3
EPS = 1e-6
NEG_INF = -1e30
SCALE = HEAD_DIM ** -0.5

LANES = 128
HALO = 8

C_SBQ = (0, 512)
C_SBKV = (512, 1024)
C_NSAQ = (1024, 1536)
C_NSAKV = (1536, 2048)
C_WIN = (2048, 2304)
C_NSAG = (2304, 2432)
C_DQ = (2432, 2944)
C_DKV = (2944, 3456)
C_MG = 3456
N_GATE_RAW = 3 * NSA_HEADS

VMEM_LIMIT = 56 * 1024 * 1024


def _cparams(sem):
    return pltpu.CompilerParams(dimension_semantics=sem, vmem_limit_bytes=VMEM_LIMIT)


def _dot(a, b):
    return jnp.dot(a, b, preferred_element_type=F32)


def _dot_nt(a, b):
    return lax.dot_general(a, b, (((1,), (1,)), ((), ())), preferred_element_type=F32)


def _rms(x, g):
    ms = jnp.mean(x * x, axis=-1, keepdims=True)
    return x * lax.rsqrt(ms + EPS) * g


def _sigmoid(x):
    return 1.0 / (1.0 + jnp.exp(-x))


def _log_keep(z):
    return -(jnp.maximum(z, 0.0) + jnp.log(1.0 + jnp.exp(-jnp.abs(z))))


def _resident(shape, index_map):
    return pl.BlockSpec(shape, index_map, pipeline_mode=pl.Buffered(1))


def _inproj_kernel(x_ref, g_ref, w_ref, cos_ref, s1_ref, s2_ref,
                   sbq_ref, sbkv_ref, sbkvb_ref, nsaq_ref, nsakv_ref, nsakvb_ref,
                   winkv_ref, winkvb_ref, nsag_ref, dq_ref, dkv_ref, dkvb_ref, mg_ref):
    tm = x_ref.shape[0]
    hb = _rms(x_ref[...], g_ref[...]).astype(BF16)
    cos, s1, s2 = cos_ref[...], s1_ref[...], s2_ref[...]
    lo_half = lax.broadcasted_iota(jnp.int32, (tm, LANES), 1) < HEAD_DIM

    def mm(c):
        return _dot(hb, w_ref[:, c[0]:c[1]])

    def rope(u):
        return u * cos + pltpu.roll(u, LANES - ROT_DIM // 2, 1) * s1 + pltpu.roll(u, ROT_DIM // 2, 1) * s2

    def tiles(u):
        return [u[:, t * LANES:(t + 1) * LANES] for t in range(u.shape[1] // LANES)]

    def padded_heads(u, half_of_head, do_rope):
        out = []
        for t, ut in enumerate(tiles(u)):
            if do_rope:
                ut = rope(ut)
            ut = ut * SCALE
            ur = pltpu.roll(ut, HEAD_DIM, 1)
            for hh in range(2):
                tgt = half_of_head(2 * t + hh)
                src = ut if tgt == hh else ur
                out.append(jnp.where(lo_half if tgt == 0 else jnp.logical_not(lo_half), src, 0.0))
        return jnp.concatenate(out, axis=1).astype(BF16)

    sbq_ref[...] = padded_heads(mm(C_SBQ), lambda h: (h // 2) % 2, False)
    u = mm(C_SBKV)
    sbkv_ref[...] = u
    sbkvb_ref[...] = u.astype(BF16)
    nsaq_ref[...] = padded_heads(mm(C_NSAQ), lambda h: h // NSA_REP, True)
    t = tiles(mm(C_NSAKV))
    u = jnp.concatenate([rope(t[0]), t[1], rope(t[2]), t[3]], axis=1)
    nsakv_ref[...] = u
    nsakvb_ref[...] = u.astype(BF16)
    t = tiles(mm(C_WIN))
    u = jnp.concatenate([rope(t[0]), t[1]], axis=1)
    winkv_ref[...] = u
    winkvb_ref[...] = u.astype(BF16)
    nsag_ref[...] = _sigmoid(mm(C_NSAG))
    dq_ref[...] = (jnp.concatenate([rope(ut) for ut in tiles(mm(C_DQ))], axis=1) * SCALE).astype(BF16)
    t = tiles(mm(C_DKV))
    u = jnp.concatenate([rope(t[0]), rope(t[1]), t[2], t[3]], axis=1)
    dkv_ref[...] = u
    dkvb_ref[...] = u.astype(BF16)
    d3 = mg_ref.shape[1]
    for c in range(3):
        mg_ref[:, c * (d3 // 3):(c + 1) * (d3 // 3)] = _sigmoid(
            mm((C_MG + c * (d3 // 3), C_MG + (c + 1) * (d3 // 3))))


def _inproj(x2d, g, w, rope_tabs, tm, n_pos_tiles):
    T, D = x2d.shape
    n_w = w.shape[1]
    row = lambda i: (i, 0)
    tab = lambda i: (i % n_pos_tiles, 0)
    widths = [(1024, BF16), (512, F32), (512, BF16), (1024, BF16), (512, F32), (512, BF16),
              (256, F32), (256, BF16), (LANES, F32), (512, BF16), (512, F32), (512, BF16), (3 * D, F32)]
    return pl.pallas_call(
        _inproj_kernel,
        grid=(T // tm,),
        in_specs=[pl.BlockSpec((tm, D), row), _resident((1, D), lambda i: (0, 0)),
                  _resident((D, n_w), lambda i: (0, 0)),
                  pl.BlockSpec((tm, LANES), tab), pl.BlockSpec((tm, LANES), tab), pl.BlockSpec((tm, LANES), tab)],
        out_specs=[pl.BlockSpec((tm, wd), row) for wd, _ in widths],
        out_shape=[jax.ShapeDtypeStruct((T, wd), dt) for wd, dt in widths],
        compiler_params=_cparams(("parallel",)),
        name="inproj",
    )(x2d, g, w, *rope_tabs)


def _rope_tables(pos):
    half = ROT_DIM // 2
    inv_freq = ROPE_THETA ** (-jnp.arange(half, dtype=F32) * 2.0 / ROT_DIM)
    ang = pos.astype(F32)[:, None] * inv_freq
    cos, sin = jnp.cos(ang), jnp.sin(ang)
    lane = jnp.arange(LANES) % HEAD_DIM
    idx = lane % half
    c = jnp.where(lane < ROT_DIM, cos[:, idx], 1.0)
    s1 = jnp.where(lane < half, -sin[:, idx], 0.0)
    s2 = jnp.where((lane >= half) & (lane < ROT_DIM), sin[:, idx], 0.0)
    return c, s1, s2


def _sb_kernel(q_ref, k_ref, v_ref, o_ref):
    tq = q_ref.shape[0]
    tk = tq
    g = pl.program_id(1)
    i = pl.program_id(2)
    rows = 2 * tq
    q2 = jnp.concatenate([q_ref[:, :LANES], q_ref[:, LANES:]], axis=0)
    tri = (lax.broadcasted_iota(jnp.int32, (tk, tk), 0) > lax.broadcasted_iota(jnp.int32, (tk, tk), 1)).astype(BF16)

    def block(j, carry, diagonal):
        acc, run = carry
        k = k_ref[pl.ds(pl.multiple_of(j * tk, tk), tk), :]
        v = v_ref[pl.ds(pl.multiple_of(j * tk, tk), tk), :]
        z = _dot_nt(q2, k)
        lk = _log_keep(z)
        if diagonal:
            r = lax.broadcasted_iota(jnp.int32, (rows, tk), 0)
            qrow = jnp.where(r >= tq, r - tq, r)
            mask = lax.broadcasted_iota(jnp.int32, (rows, tk), 1) < qrow
            lk = jnp.where(mask, lk, 0.0)
        hi = lk.astype(BF16)
        lo = (lk - hi.astype(F32)).astype(BF16)
        between = _dot(hi, tri) + _dot(lo, tri)
        e = jnp.exp(lk + z + between + run)
        if diagonal:
            e = jnp.where(mask, e, 0.0)
        acc = acc + _dot(e.astype(BF16), v)
        run = run + jnp.sum(lk, axis=1, keepdims=True)
        return acc, run

    carry = block(i, (jnp.zeros((rows, LANES), F32), jnp.zeros((rows, 1), F32)), True)
    acc, _ = lax.fori_loop(0, i, lambda t, c: block(i - 1 - t, c, False), carry)
    acc_a, acc_b = acc[:tq], acc[tq:]
    odd = (g % 2) == 1
    lo_half = lax.broadcasted_iota(jnp.int32, (tq, LANES), 1) < HEAD_DIM
    first = jnp.where(odd, pltpu.roll(acc_a, HEAD_DIM, 1), acc_a)
    second = jnp.where(odd, acc_b, pltpu.roll(acc_b, HEAD_DIM, 1))
    o_ref[...] = jnp.where(lo_half, first, second).astype(o_ref.dtype)


def _sb_prompt(q_pad, kv_b, B, S, tq):
    T = B * S
    nq = S // tq
    return pl.pallas_call(
        _sb_kernel,
        grid=(B, SB_KV, nq),
        in_specs=[pl.BlockSpec((tq, 2 * LANES), lambda b, g, i: (b * nq + i, g)),
                  pl.BlockSpec((S, LANES), lambda b, g, i: (b, g // 2)),
                  pl.BlockSpec((S, LANES), lambda b, g, i: (b, 2 + g // 2))],
        out_specs=pl.BlockSpec((tq, LANES), lambda b, g, i: (b * nq + i, g)),
        out_shape=jax.ShapeDtypeStruct((T, SB_HEADS * HEAD_DIM), BF16),
        compiler_params=_cparams(("parallel", "parallel", "arbitrary")),
        name="sb_prompt",
    )(q_pad, kv_b, kv_b)


def _diff_lambda(lamv_ref, c_ref):
    lv = lamv_ref[...]
    a = jnp.sum(lv[0:1] * lv[1:2], axis=1, keepdims=True)
    b = jnp.sum(lv[2:3] * lv[3:4], axis=1, keepdims=True)
    lam_init = c_ref[0:1, 0:1]
    return jnp.exp(a) - jnp.exp(b) + lam_init, lam_init


def _softmax_step(s, mask, carry, v):
    m, l, acc = carry
    if mask is not None:
        s = jnp.where(mask, s, NEG_INF)
    m_new = jnp.maximum(m, jnp.max(s, axis=1, keepdims=True))
    p = jnp.exp(s - m_new)
    if mask is not None:
        p = jnp.where(mask, p, 0.0)
    alpha = jnp.exp(m - m_new)
    l = alpha * l + jnp.sum(p, axis=1, keepdims=True)
    acc = alpha * acc + _dot(p.astype(BF16), v)
    return m_new, l, acc


def _softmax_init(rows, width):
    return (jnp.full((rows, 1), NEG_INF, F32), jnp.zeros((rows, 1), F32), jnp.zeros((rows, width), F32))


def _diff_kernel(q_ref, k_ref, v_ref, lamv_ref, subln_ref, c_ref, o_ref):
    tq = q_ref.shape[0]
    tk = tq
    i = pl.program_id(2)
    rows = 4 * tq
    lo_half = lax.broadcasted_iota(jnp.int32, (tq, LANES), 1) < HEAD_DIM
    qs = []
    for m in range(2):
        for h in range(2):
            qh = q_ref[:, h * LANES:(h + 1) * LANES].astype(F32)
            qs.append(jnp.where(lo_half if m == 0 else jnp.logical_not(lo_half), qh, 0.0).astype(BF16))
    q4 = jnp.concatenate(qs, axis=0)

    def block(j, carry, diagonal):
        k = k_ref[pl.ds(pl.multiple_of(j * tk, tk), tk), :]
        v = v_ref[pl.ds(pl.multiple_of(j * tk, tk), tk), :]
        s = _dot_nt(q4, k)
        mask = None
        if diagonal:
            r = lax.broadcasted_iota(jnp.int32, (rows, tk), 0)
            mask = lax.broadcasted_iota(jnp.int32, (rows, tk), 1) <= (r % tq)
        return _softmax_step(s, mask, carry, v)

    carry = block(i, _softmax_init(rows, LANES), True)
    _, l, acc = lax.fori_loop(0, i, lambda t, c: block(i - 1 - t, c, False), carry)
    a = acc / l
    lam, lam_init = _diff_lambda(lamv_ref, c_ref)
    outs = []
    for h in range(2):
        o = a[h * tq:(h + 1) * tq] - lam * a[(2 + h) * tq:(3 + h) * tq]
        outs.append(_rms(o, subln_ref[...]) * (1.0 - lam_init))
    o_ref[...] = jnp.concatenate(outs, axis=1).astype(o_ref.dtype)


def _diff_prompt(q_b, kv_b, lamv, subln, consts, B, S, tq):
    T = B * S
    nq = S // tq
    whole = lambda b, g, i: (0, 0)
    return pl.pallas_call(
        _diff_kernel,
        grid=(B, DIFF_KV, nq),
        in_specs=[pl.BlockSpec((tq, 2 * LANES), lambda b, g, i: (b * nq + i, g)),
                  pl.BlockSpec((S, LANES), lambda b, g, i: (b, g)),
                  pl.BlockSpec((S, LANES), lambda b, g, i: (b, 2 + g)),
                  pl.BlockSpec(lamv.shape, whole), pl.BlockSpec(subln.shape, whole),
                  pl.BlockSpec(consts.shape, whole)],
        out_specs=pl.BlockSpec((tq, 2 * LANES), lambda b, g, i: (b * nq + i, g)),
        out_shape=jax.ShapeDtypeStruct((T, DIFF_HEADS * DIFF_VDIM), BF16),
        compiler_params=_cparams(("parallel", "parallel", "arbitrary")),
        name="diff_prompt",
    )(q_b, kv_b, kv_b, lamv, subln, consts)


def _compress_kernel(*refs, n_prefetch):
    x_ref, pe_ref, wk_ref, wv_ref, o_ref, buf_k, buf_v = refs[n_prefetch:]
    t = pl.program_id(1)
    rows_blk = x_ref.shape[0]
    nblk = buf_k.shape[0] // NSA_BLOCK
    dst = pl.ds(pl.multiple_of(t * rows_blk, rows_blk), rows_blk)
    buf_k[dst, :] = x_ref[:, :LANES]
    buf_v[dst, :] = x_ref[:, LANES:]

    @pl.when(t == pl.num_programs(1) - 1)
    def _():
        def body(j, acc):
            pe = pe_ref[pl.ds(j, 1), :]
            xk = (buf_k[pl.ds(j, nblk, stride=NSA_BLOCK), :] + pe).astype(BF16)
            xv = (buf_v[pl.ds(j, nblk, stride=NSA_BLOCK), :] + pe).astype(BF16)
            return acc + jnp.concatenate([_dot(xk, wk_ref[j]), _dot(xv, wv_ref[j])], axis=1)

        o_ref[...] = lax.fori_loop(0, NSA_BLOCK, body, jnp.zeros((nblk, 2 * LANES), F32)).astype(o_ref.dtype)


def _compress_weights(pe, wk, wv):
    eye = jnp.eye(NSA_KV, dtype=F32)
    bd = lambda w: jnp.einsum('ab,jde->jadbe', eye, w).reshape(NSA_BLOCK, LANES, LANES).astype(BF16)
    return jnp.tile(pe, (1, NSA_KV)), bd(wk), bd(wv)


def _compress_prompt(nsakv, pe2, wk2, wv2, B, S):
    chunk = min(S, 2048)
    nc = S // chunk
    whole2 = lambda b, t: (0, 0)
    whole3 = lambda b, t: (0, 0, 0)
    return pl.pallas_call(
        functools.partial(_compress_kernel, n_prefetch=0),
        grid=(B * nc, 1),
        in_specs=[pl.BlockSpec((chunk, 2 * LANES), lambda b, t: (b, 0)),
                  pl.BlockSpec(pe2.shape, whole2), pl.BlockSpec(wk2.shape, whole3), pl.BlockSpec(wv2.shape, whole3)],
        out_specs=pl.BlockSpec((chunk // NSA_BLOCK, 2 * LANES), lambda b, t: (b, 0)),
        out_shape=jax.ShapeDtypeStruct((B * S // NSA_BLOCK, 2 * LANES), BF16),
        scratch_shapes=[pltpu.VMEM((chunk, LANES), F32)] * 2,
        compiler_params=_cparams(("parallel", "arbitrary")),
        name="compress_prompt",
    )(nsakv, pe2, wk2, wv2)


def _compress_decode(cache4, layer, page_table, pe2, wk2, wv2, seqs_per_step):
    nb, npg = page_table.shape
    page = cache4.shape[2]
    sg = seqs_per_step
    rows = sg * npg * page
    whole2 = lambda o, t, pt: (0, 0)
    whole3 = lambda o, t, pt: (0, 0, 0)
    grid_spec = pltpu.PrefetchScalarGridSpec(
        num_scalar_prefetch=1,
        grid=(nb // sg, sg * npg),
        in_specs=[pl.BlockSpec((None, None, page, 2 * LANES),
                               lambda o, t, pt: (layer, pt[o * sg + t // npg, t % npg], 0, 0)),
                  pl.BlockSpec(pe2.shape, whole2), pl.BlockSpec(wk2.shape, whole3), pl.BlockSpec(wv2.shape, whole3)],
        out_specs=pl.BlockSpec((rows // NSA_BLOCK, 2 * LANES), lambda o, t, pt: (o, 0)),
        scratch_shapes=[pltpu.VMEM((rows, LANES), F32)] * 2,
    )
    return pl.pallas_call(
        functools.partial(_compress_kernel, n_prefetch=1),
        grid_spec=grid_spec,
        out_shape=jax.ShapeDtypeStruct((nb * npg * page // NSA_BLOCK, 2 * LANES), BF16),
        compiler_params=_cparams(("parallel", "arbitrary")),
        name="compress_decode",
    )(page_table, cache4, pe2, wk2, wv2)


def _top_blocks(imp, n_top):
    nb = imp.shape[1]
    col = lax.broadcasted_iota(jnp.int32, imp.shape, 1)

    def body(_, carry):
        vals, sel = carry
        m = jnp.max(vals, axis=1, keepdims=True)
        idx = jnp.min(jnp.where(vals == m, col, nb), axis=1, keepdims=True)
        hit = col == idx
        sel = jnp.where(jnp.logical_and(hit, m >= 0.0), 1.0, sel)
        vals = jnp.where(hit, -2.0, vals)
        return vals, sel

    _, sel = lax.fori_loop(0, n_top, body, (imp, jnp.zeros(imp.shape, F32)))
    return sel


def _compressed_branch(q, kc, vc, n_valid_of_row, cur_of_row, group_rows):
    sc = _dot_nt(q, kc)
    col = lax.broadcasted_iota(jnp.int32, sc.shape, 1)
    cmask = col < n_valid_of_row
    mx = jnp.max(jnp.where(cmask, sc, NEG_INF), axis=1, keepdims=True)
    e = jnp.where(cmask, jnp.exp(sc - mx), 0.0)
    l = jnp.sum(e, axis=1, keepdims=True)
    pc = e / jnp.where(l > 0.0, l, 1.0)
    return _dot(pc.astype(BF16), vc), pc


def _gate(gates, head, branch, rows):
    c = head * 3 + branch
    return gates[:, c:c + 1]


def _nsa_kernel(q_ref, gate_ref, cb_ref, sel_ref, win_ref, o_ref, *, n_top):
    tq = q_ref.shape[0]
    nb = cb_ref.shape[0]
    tk = min(2 * LANES, sel_ref.shape[0])
    i = pl.program_id(1)
    q0 = i * tq
    rows = NSA_REP * tq
    pos1 = q0 + lax.broadcasted_iota(jnp.int32, (tq, 1), 0)
    pos4 = q0 + lax.broadcasted_iota(jnp.int32, (rows, 1), 0) % tq
    kc, vc = cb_ref[:, :LANES], cb_ref[:, LANES:]
    q4s, o_cmps, imps = [], [], []
    for g in range(NSA_KV):
        q4 = jnp.concatenate([q_ref[:, (NSA_REP * g + r) * LANES:(NSA_REP * g + r + 1) * LANES]
                              for r in range(NSA_REP)], axis=0)
        o_cmp, pc = _compressed_branch(q4, kc, vc, (pos4 + 1) // NSA_BLOCK, None, None)
        imp = pc[0:tq]
        for r in range(1, NSA_REP):
            imp = imp + pc[r * tq:(r + 1) * tq]
        q4s.append(q4)
        o_cmps.append(o_cmp)
        imps.append(imp)
    cur2 = jnp.concatenate([pos1, pos1], axis=0) // NSA_BLOCK
    col2 = lax.broadcasted_iota(jnp.int32, (2 * tq, nb), 1)
    imp2 = jnp.where(col2 < cur2, jnp.concatenate(imps, axis=0), -1.0)
    sel2 = jnp.where(col2 == cur2, 1.0, _top_blocks(imp2, n_top))

    gates = gate_ref[...]
    lo_half = lax.broadcasted_iota(jnp.int32, (tq, LANES), 1) < HEAD_DIM
    n_sel_blocks = (q0 + tq - 1) // tk + 1
    w_lo = jnp.maximum(q0 - NSA_WINDOW, 0) // tk
    out_tiles = []
    for g in range(NSA_KV):
        q4 = q4s[g]
        m4 = jnp.concatenate([sel2[g * tq:(g + 1) * tq]] * NSA_REP, axis=0).astype(BF16)

        def sel_block(j, carry):
            k0 = pl.multiple_of(j * tk, tk)
            k = sel_ref[pl.ds(k0, tk), :LANES]
            v = sel_ref[pl.ds(k0, tk), LANES:]
            s = _dot_nt(q4, k)
            kpos_e = k0 + lax.broadcasted_iota(jnp.int32, (nb, tk), 1)
            expand = (lax.broadcasted_iota(jnp.int32, (nb, tk), 0) == kpos_e // NSA_BLOCK).astype(BF16)
            chosen = _dot(m4, expand) > 0.5
            kpos = k0 + lax.broadcasted_iota(jnp.int32, (rows, tk), 1)
            return _softmax_step(s, jnp.logical_and(chosen, kpos <= pos4), carry, v)

        _, l, acc = lax.fori_loop(0, n_sel_blocks, sel_block, _softmax_init(rows, LANES))
        o_sel = acc / l

        def win_block(j, carry):
            k0 = pl.multiple_of(j * tk, tk)
            k = win_ref[pl.ds(k0, tk), :LANES]
            v = win_ref[pl.ds(k0, tk), LANES:]
            s = _dot_nt(q4, k)
            kpos = k0 + lax.broadcasted_iota(jnp.int32, (rows, tk), 1)
            return _softmax_step(s, jnp.logical_and(kpos <= pos4, kpos >= pos4 - NSA_WINDOW), carry, v)

        _, l, acc = lax.fori_loop(w_lo, n_sel_blocks, win_block, _softmax_init(rows, LANES))
        o_win = acc / l

        heads = []
        for r in range(NSA_REP):
            h = NSA_REP * g + r
            sl = slice(r * tq, (r + 1) * tq)
            heads.append(_gate(gates, h, 0, tq) * o_cmps[g][sl] + _gate(gates, h, 1, tq) * o_sel[sl]
                         + _gate(gates, h, 2, tq) * o_win[sl])
        for t in range(NSA_REP // 2):
            a, b = heads[2 * t], heads[2 * t + 1]
            if g == 0:
                out_tiles.append(jnp.where(lo_half, a, pltpu.roll(b, HEAD_DIM, 1)))
            else:
                out_tiles.append(jnp.where(lo_half, pltpu.roll(a, HEAD_DIM, 1), b))
    o_ref[...] = jnp.concatenate(out_tiles, axis=1).astype(o_ref.dtype)


def _nsa_prompt(q_pad, gates, cb, nsakv_b, winkv_b, B, S, tq):
    T = B * S
    nq = S // tq
    nb = S // NSA_BLOCK
    return pl.pallas_call(
        functools.partial(_nsa_kernel, n_top=min(NSA_SEL - 1, nb)),
        grid=(B, nq),
        in_specs=[pl.BlockSpec((tq, NSA_HEADS * LANES), lambda b, i: (b * nq + i, 0)),
                  pl.BlockSpec((tq, LANES), lambda b, i: (b * nq + i, 0)),
                  pl.BlockSpec((nb, 2 * LANES), lambda b, i: (b, 0)),
                  pl.BlockSpec((S, 2 * LANES), lambda b, i: (b, 1)),
                  pl.BlockSpec((S, 2 * LANES), lambda b, i: (b, 0))],
        out_specs=pl.BlockSpec((tq, NSA_HEADS * HEAD_DIM), lambda b, i: (b * nq + i, 0)),
        out_shape=jax.ShapeDtypeStruct((T, NSA_HEADS * HEAD_DIM), BF16),
        compiler_params=_cparams(("parallel", "arbitrary")),
        name="nsa_prompt",
    )(q_pad, gates, cb, nsakv_b, winkv_b)


def _merge_kernel(x_ref, mg_ref, osb_ref, onsa_ref, odiff_ref, wsb_ref, wnsa_ref, wdiff_ref, wo_ref, y_ref):
    d = x_ref.shape[1]
    merged = (mg_ref[:, 0:d] * _dot(osb_ref[...].astype(BF16), wsb_ref[...])
              + mg_ref[:, d:2 * d] * _dot(onsa_ref[...].astype(BF16), wnsa_ref[...])
              + mg_ref[:, 2 * d:3 * d] * _dot(odiff_ref[...].astype(BF16), wdiff_ref[...]))
    y_ref[...] = x_ref[...] + _dot(merged.astype(BF16), wo_ref[...])


def _merge(x2d, mg, o_sb, o_nsa, o_diff, w_sb, w_nsa, w_diff, w_o, tm):
    T, D = x2d.shape
    row = lambda i: (i, 0)
    whole = lambda i: (0, 0)
    return pl.pallas_call(
        _merge_kernel,
        grid=(T // tm,),
        in_specs=[pl.BlockSpec((tm, D), row), pl.BlockSpec((tm, 3 * D), row),
                  pl.BlockSpec((tm, o_sb.shape[1]), row), pl.BlockSpec((tm, o_nsa.shape[1]), row),
                  pl.BlockSpec((tm, o_diff.shape[1]), row),
                  _resident(w_sb.shape, whole), _resident(w_nsa.shape, whole),
                  _resident(w_diff.shape, whole), _resident(w_o.shape, whole)],
        out_specs=pl.BlockSpec((tm, D), row),
        out_shape=jax.ShapeDtypeStruct((T, D), F32),
        compiler_params=_cparams(("parallel",)),
        name="merge",
    )(x2d, mg, o_sb, o_nsa, o_diff, w_sb, w_nsa, w_diff, w_o)


def _gelu(c):
    return 0.5 * c * (1.0 + jnp.tanh(math.sqrt(2.0 / math.pi) * (c + 0.044715 * (c * c * c))))


def _ffn_prompt_kernel(x_ref, halo_ref, g_ref, wup_ref, cw_ref, cb_ref, wdown_ref, gf_ref,
                       y_ref, tail_ref, yn_ref, a_scr, *, tiles_per_seq):
    tm = x_ref.shape[0]
    dff = cw_ref.shape[1]
    i = pl.program_id(0)
    x = x_ref[...]
    hb = _rms(x, g_ref[...]).astype(BF16)
    a = _dot(hb, wup_ref[:, :dff])
    gate = _dot(hb, wup_ref[:, dff:])
    a_halo = _dot(_rms(halo_ref[...], g_ref[...]).astype(BF16), wup_ref[:, :dff])
    a_scr[0:HALO, :] = jnp.where(i % tiles_per_seq == 0, 0.0, a_halo)
    a_scr[HALO:HALO + tm, :] = a
    c = cb_ref[...] + a * cw_ref[CONV_W - 1:CONV_W, :]
    for j in range(CONV_W - 1):
        off = HALO - (CONV_W - 1) + j
        c = c + a_scr[off:off + tm, :] * cw_ref[j:j + 1, :]
    y = x + _dot((_gelu(c) * gate).astype(BF16), wdown_ref[...])
    y_ref[...] = y
    yn_ref[...] = _rms(y, gf_ref[...])
    tail_ref[...] = a_scr[tm:tm + HALO, :]


def _ffn_prompt(x2d, g, w_up, conv_w, conv_b, w_down, g_final, S, tm):
    T, D = x2d.shape
    dff = conv_w.shape[1]
    whole = lambda i: (0, 0)
    hb = tm // HALO
    return pl.pallas_call(
        functools.partial(_ffn_prompt_kernel, tiles_per_seq=S // tm),
        grid=(T // tm,),
        in_specs=[pl.BlockSpec((tm, D), lambda i: (i, 0)),
                  pl.BlockSpec((HALO, D), lambda i: (jnp.maximum(i * hb - 1, 0), 0)),
                  _resident((1, D), whole), _resident(w_up.shape, whole), _resident(conv_w.shape, whole),
                  _resident(conv_b.shape, whole), _resident(w_down.shape, whole), _resident((1, D), whole)],
        out_specs=[pl.BlockSpec((tm, D), lambda i: (i, 0)),
                   pl.BlockSpec((None, HALO, dff), lambda i: (i, 0, 0)),
                   pl.BlockSpec((tm, D), lambda i: (i, 0))],
        out_shape=[jax.ShapeDtypeStruct((T, D), F32), jax.ShapeDtypeStruct((T // tm, HALO, dff), F32),
                   jax.ShapeDtypeStruct((T, D), F32)],
        scratch_shapes=[pltpu.VMEM((tm + HALO, dff), F32)],
        compiler_params=_cparams(("parallel",)),
        name="ffn_prompt",
    )(x2d, x2d, g, w_up, conv_w, conv_b, w_down, g_final)


def _ffn_sample_kernel(x_ref, p0_ref, p1_ref, g_ref, wup_ref, cw_ref, cb_ref, wdown_ref, gf_ref,
                       y_ref, a_ref, yn_ref):
    dff = cw_ref.shape[1]
    x = x_ref[...]
    hb = _rms(x, g_ref[...]).astype(BF16)
    a = _dot(hb, wup_ref[:, :dff])
    gate = _dot(hb, wup_ref[:, dff:])
    c = cb_ref[...] + p0_ref[...] * cw_ref[0:1, :] + p1_ref[...] * cw_ref[1:2, :] + a * cw_ref[2:3, :]
    y = x + _dot((_gelu(c) * gate).astype(BF16), wdown_ref[...])
    y_ref[...] = y
    a_ref[...] = a
    yn_ref[...] = _rms(y, gf_ref[...])


def _ffn_sample(x2d, prev0, prev1, g, w_up, conv_w, conv_b, w_down, g_final):
    T, D = x2d.shape
    dff = conv_w.shape[1]
    return pl.pallas_call(
        _ffn_sample_kernel,
        out_shape=[jax.ShapeDtypeStruct((T, D), F32), jax.ShapeDtypeStruct((T, dff), F32),
                   jax.ShapeDtypeStruct((T, D), F32)],
        compiler_params=pltpu.CompilerParams(vmem_limit_bytes=VMEM_LIMIT),
        name="ffn_sample",
    )(x2d, prev0, prev1, g, w_up, conv_w, conv_b, w_down, g_final)


def _head_rows(q_ref, split):
    q8 = q_ref[...].astype(F32)
    row = lax.broadcasted_iota(jnp.int32, q8.shape, 0)
    return jnp.concatenate([jnp.where(row < split, q8, 0.0), jnp.where(row >= split, q8, 0.0)], axis=1)


def _sb_decode_kernel(pt_ref, q_ref, page_ref, o_ref, acc_scr, run_scr):
    p = pl.program_id(1)
    tk = page_ref.shape[0]
    half = page_ref.shape[1] // 2

    @pl.when(p == 0)
    def _():
        acc_scr[...] = jnp.zeros_like(acc_scr)
        run_scr[...] = jnp.zeros_like(run_scr)

    q = _head_rows(q_ref, SB_HEADS // 2).astype(BF16)
    k = page_ref[:, :half].astype(BF16)
    v = page_ref[:, half:].astype(BF16)
    z = _dot_nt(q, k)
    lk = _log_keep(z)
    tri = (lax.broadcasted_iota(jnp.int32, (tk, tk), 0) > lax.broadcasted_iota(jnp.int32, (tk, tk), 1)).astype(BF16)
    hi = lk.astype(BF16)
    lo = (lk - hi.astype(F32)).astype(BF16)
    e = jnp.exp(lk + z + _dot(hi, tri) + _dot(lo, tri) + run_scr[:, 0:1])
    acc_scr[...] += _dot(e.astype(BF16), v)
    run_scr[...] += jnp.sum(lk, axis=1, keepdims=True)

    @pl.when(p == pl.num_programs(1) - 1)
    def _():
        row = lax.broadcasted_iota(jnp.int32, acc_scr.shape, 0)
        lane = lax.broadcasted_iota(jnp.int32, acc_scr.shape, 1)
        o_ref[...] = jnp.where(lane // HEAD_DIM == row // 2, acc_scr[...], 0.0)


def _sb_decode(cache4, layer, page_table, q_pad3):
    nb, npg = page_table.shape
    page, width = cache4.shape[2], cache4.shape[3]
    grid_spec = pltpu.PrefetchScalarGridSpec(
        num_scalar_prefetch=1,
        grid=(nb, npg),
        in_specs=[pl.BlockSpec((None, SB_HEADS, LANES), lambda b, p, pt: (b, 0, 0)),
                  pl.BlockSpec((None, None, page, width), lambda b, p, pt: (layer, pt[b, npg - 1 - p], 0, 0))],
        out_specs=pl.BlockSpec((None, SB_HEADS, width // 2), lambda b, p, pt: (b, 0, 0)),
        scratch_shapes=[pltpu.VMEM((SB_HEADS, width // 2), F32), pltpu.VMEM((SB_HEADS, LANES), F32)],
    )
    return pl.pallas_call(
        _sb_decode_kernel,
        grid_spec=grid_spec,
        out_shape=jax.ShapeDtypeStruct((nb, SB_HEADS, width // 2), F32),
        compiler_params=_cparams(("parallel", "arbitrary")),
        name="sb_decode",
    )(page_table, q_pad3, cache4)


def _scratch_softmax_step(s, mask, v, m_scr, l_scr, acc_scr):
    carry = (m_scr[:, 0:1], l_scr[:, 0:1], acc_scr[...])
    m, l, acc = _softmax_step(s, mask, carry, v)
    m_scr[...] = jnp.broadcast_to(m, m_scr.shape)
    l_scr[...] = jnp.broadcast_to(l, l_scr.shape)
    acc_scr[...] = acc


def _new_key_step(qf, k_new, v_new, carry):
    m, l, acc = carry
    s = jnp.sum(qf * k_new, axis=1, keepdims=True)
    m_new = jnp.maximum(m, s)
    alpha = jnp.exp(m - m_new)
    p = jnp.exp(s - m_new)
    return m_new, alpha * l + p, alpha * acc + p.astype(BF16).astype(F32) * v_new


def _diff_decode_kernel(pt_ref, q_ref, page_ref, new_ref, lamv_ref, subln_ref, c_ref, o_ref, m_scr, l_scr, acc_scr):
    p = pl.program_id(1)
    half = page_ref.shape[1] // 2

    @pl.when(p == 0)
    def _():
        m_scr[...] = jnp.full_like(m_scr, NEG_INF)
        l_scr[...] = jnp.zeros_like(l_scr)
        acc_scr[...] = jnp.zeros_like(acc_scr)

    row = lax.broadcasted_iota(jnp.int32, (2 * DIFF_HEADS, LANES), 0)
    q8 = q_ref[...].astype(F32)
    qf = jnp.concatenate([jnp.where((row % DIFF_HEADS) < 2, q8, 0.0), jnp.where((row % DIFF_HEADS) >= 2, q8, 0.0)], axis=1)
    k = page_ref[:, :half].astype(BF16)
    v = page_ref[:, half:].astype(BF16)
    _scratch_softmax_step(_dot_nt(qf.astype(BF16), k), None, v, m_scr, l_scr, acc_scr)

    @pl.when(p == pl.num_programs(1) - 1)
    def _():
        k_new = new_ref[:, :half].astype(BF16).astype(F32)
        v_new = new_ref[:, half:].astype(BF16).astype(F32)
        _, l, acc = _new_key_step(qf, k_new, v_new, (m_scr[:, 0:1], l_scr[:, 0:1], acc_scr[...]))
        a = acc / l
        lam, lam_init = _diff_lambda(lamv_ref, c_ref)
        o = a - lam * pltpu.roll(a, DIFF_HEADS, 0)
        r2 = lax.broadcasted_iota(jnp.int32, o.shape, 0)
        lane = lax.broadcasted_iota(jnp.int32, o.shape, 1)
        valid = jnp.logical_and(r2 < DIFF_HEADS, lane // DIFF_VDIM == r2 // 2)
        o = jnp.where(valid, o, 0.0)
        ms = jnp.sum(o * o, axis=1, keepdims=True) * (1.0 / DIFF_VDIM)
        g2 = jnp.concatenate([subln_ref[...], subln_ref[...]], axis=1)
        o_ref[...] = o * lax.rsqrt(ms + EPS) * g2 * (1.0 - lam_init)


def _diff_decode(cache4, layer, page_table, q8, new_rows3, lamv, subln, consts):
    nb, npg = page_table.shape
    page, width = cache4.shape[2], cache4.shape[3]
    whole = lambda b, p, pt: (0, 0)
    grid_spec = pltpu.PrefetchScalarGridSpec(
        num_scalar_prefetch=1,
        grid=(nb, npg),
        in_specs=[pl.BlockSpec((None, 2 * DIFF_HEADS, LANES), lambda b, p, pt: (b, 0, 0)),
                  pl.BlockSpec((None, None, page, width), lambda b, p, pt: (layer, pt[b, p], 0, 0)),
                  pl.BlockSpec((None, 1, width), lambda b, p, pt: (b, 0, 0)),
                  pl.BlockSpec(lamv.shape, whole), pl.BlockSpec(subln.shape, whole), pl.BlockSpec(consts.shape, whole)],
        out_specs=pl.BlockSpec((None, 2 * DIFF_HEADS, width // 2), lambda b, p, pt: (b, 0, 0)),
        scratch_shapes=[pltpu.VMEM((2 * DIFF_HEADS, LANES), F32), pltpu.VMEM((2 * DIFF_HEADS, LANES), F32),
                        pltpu.VMEM((2 * DIFF_HEADS, width // 2), F32)],
    )
    return pl.pallas_call(
        _diff_decode_kernel,
        grid_spec=grid_spec,
        out_shape=jax.ShapeDtypeStruct((nb, 2 * DIFF_HEADS, width // 2), F32),
        compiler_params=_cparams(("parallel", "arbitrary")),
        name="diff_decode",
    )(page_table, q8, cache4, new_rows3, lamv, subln, consts)


def _nsa_decode_kernel(pt_ref, q_ref, gate_ref, cb_ref, page_ref, new_ref, win_ref, wnew_ref, o_ref,
                       sel_scr, cmp_scr, m_scr, l_scr, acc_scr, *, n_top, n_cmp):
    p = pl.program_id(1)
    tk = page_ref.shape[0]
    rows = NSA_HEADS
    q = q_ref[...]
    qf = q.astype(F32)
    row1 = lax.broadcasted_iota(jnp.int32, (rows, 1), 0)

    @pl.when(p == 0)
    def _():
        pad = jnp.zeros((LANES - n_cmp, 2 * LANES), BF16)
        cb = jnp.concatenate([cb_ref[...], pad], axis=0)
        o_cmp, pc = _compressed_branch(q, cb[:, :LANES], cb[:, LANES:], n_cmp, None, None)
        cmp_scr[...] = o_cmp
        rowf = lax.broadcasted_iota(jnp.int32, pc.shape, 0)
        imp0 = jnp.sum(jnp.where(rowf < NSA_REP, pc, 0.0), axis=0, keepdims=True)
        imp1 = jnp.sum(jnp.where(rowf >= NSA_REP, pc, 0.0), axis=0, keepdims=True)
        col = lax.broadcasted_iota(jnp.int32, pc.shape, 1)
        imp = jnp.where(col < n_cmp, jnp.where(rowf < NSA_REP, imp0, imp1), -1.0)
        sel_scr[...] = _top_blocks(imp, n_top)
        m_scr[...] = jnp.full_like(m_scr, NEG_INF)
        l_scr[...] = jnp.zeros_like(l_scr)
        acc_scr[...] = jnp.zeros_like(acc_scr)

    k = page_ref[:, :LANES].astype(BF16)
    v = page_ref[:, LANES:].astype(BF16)
    s = _dot_nt(q, k)
    blk = (p * tk + lax.broadcasted_iota(jnp.int32, (LANES, tk), 1)) // NSA_BLOCK
    expand = (lax.broadcasted_iota(jnp.int32, (LANES, tk), 0) == blk).astype(BF16)
    chosen = _dot(sel_scr[...].astype(BF16), expand) > 0.5
    _scratch_softmax_step(s, chosen, v, m_scr, l_scr, acc_scr)

    @pl.when(p == pl.num_programs(1) - 1)
    def _():
        bf = lambda x: x.astype(BF16).astype(F32)
        _, l, acc = _new_key_step(qf, bf(new_ref[:, 2 * LANES:3 * LANES]), bf(new_ref[:, 3 * LANES:]),
                                  (m_scr[:, 0:1], l_scr[:, 0:1], acc_scr[...]))
        o_sel = acc / l
        wk = win_ref[:, :LANES].astype(BF16)
        wv = win_ref[:, LANES:].astype(BF16)
        carry = _softmax_step(_dot_nt(q, wk), None, _softmax_init(rows, LANES), wv)
        _, l, acc = _new_key_step(qf, bf(wnew_ref[:, :LANES]), bf(wnew_ref[:, LANES:]), carry)
        o_win = acc / l
        gates = jnp.broadcast_to(gate_ref[...], (rows, LANES))
        lane = lax.broadcasted_iota(jnp.int32, (rows, LANES), 1)
        gsel = lambda c: jnp.sum(jnp.where(lane == 3 * row1 + c, gates, 0.0), axis=1, keepdims=True)
        o = gsel(0) * cmp_scr[...] + gsel(1) * o_sel + gsel(2) * o_win
        o_ref[...] = jnp.where(lane // HEAD_DIM == row1 // NSA_REP, o, 0.0)


def _nsa_decode(cache4, layer, page_table, q8, gates3, cb3, new_rows3, win_state, wnew3):
    nb, npg = page_table.shape
    page = cache4.shape[2]
    n_cmp = cb3.shape[1]
    wlen = win_state.shape[2]
    n_top = min(NSA_SEL - 1, n_cmp + 1)
    grid_spec = pltpu.PrefetchScalarGridSpec(
        num_scalar_prefetch=1,
        grid=(nb, npg),
        in_specs=[pl.BlockSpec((None, NSA_HEADS, LANES), lambda b, p, pt: (b, 0, 0)),
                  pl.BlockSpec((None, 1, LANES), lambda b, p, pt: (b, 0, 0)),
                  pl.BlockSpec((None, n_cmp, 2 * LANES), lambda b, p, pt: (b, 0, 0)),
                  pl.BlockSpec((None, None, page, 2 * LANES), lambda b, p, pt: (layer, pt[b, p], 0, 1)),
                  pl.BlockSpec((None, 1, 4 * LANES), lambda b, p, pt: (b, 0, 0)),
                  pl.BlockSpec((None, None, wlen, 2 * LANES), lambda b, p, pt: (layer, b, 0, 0)),
                  pl.BlockSpec((None, 1, 2 * LANES), lambda b, p, pt: (b, 0, 0))],
        out_specs=pl.BlockSpec((None, NSA_HEADS, LANES), lambda b, p, pt: (b, 0, 0)),
        scratch_shapes=[pltpu.VMEM((NSA_HEADS, LANES), F32)] * 5,
    )
    return pl.pallas_call(
        functools.partial(_nsa_decode_kernel, n_top=n_top, n_cmp=n_cmp),
        grid_spec=grid_spec,
        out_shape=jax.ShapeDtypeStruct((nb, NSA_HEADS, LANES), F32),
        compiler_params=_cparams(("parallel", "arbitrary")),
        name="nsa_decode",
    )(page_table, q8, gates3, cb3, cache4, new_rows3, win_state, wnew3)


def _prep_w_in(w_in):
    g0 = C_NSAG[0]
    gate = w_in[..., g0:g0 + N_GATE_RAW]
    gate = jnp.pad(gate, ((0, 0), (0, 0), (0, LANES - N_GATE_RAW)))
    return jnp.concatenate([w_in[..., :g0], gate, w_in[..., g0 + N_GATE_RAW:]], axis=-1).astype(BF16)


def _padded_out_weight(w, n_rows, row_width, head_width, lane_of_head):
    depth, _, d = w.shape
    out = jnp.zeros((depth, n_rows * row_width, d), w.dtype)
    for h in range(w.shape[1] // head_width):
        start = h * row_width + lane_of_head(h)
        out = out.at[:, start:start + head_width].set(w[:, h * head_width:(h + 1) * head_width])
    return out


def kernel(x_prompt, x_sample, cache_sb_kv, cache_nsa_kv, cache_diff_kv, state_nsa_win, state_ffn_conv, page_table, ln_attn, w_in, nsa_cmp_pe, nsa_cmp_wk, nsa_cmp_wv, diff_lambda, diff_subln, w_sb_out, w_nsa_out, w_diff_out, w_o, ln_ffn, w_ff_up, ffn_conv_w, ffn_conv_b, w_ff_down, ln_final):
    B, S, D = x_prompt.shape
    NB_S = x_sample.shape[0]
    depth = w_in.shape[0]
    npg = page_table.shape[1]
    page = cache_sb_kv.shape[2]
    past = npg * page
    dff = ffn_conv_w.shape[2]
    T = B * S
    tq = min(256, S)
    tq_nsa = min(128, S)
    tm = min(256, S)

    w_in_b = _prep_w_in(w_in)
    w_sb_b, w_nsa_b, w_diff_b = w_sb_out.astype(BF16), w_nsa_out.astype(BF16), w_diff_out.astype(BF16)
    w_o_b, w_up_b, w_down_b = w_o.astype(BF16), w_ff_up.astype(BF16), w_ff_down.astype(BF16)
    w_sb_s = _padded_out_weight(w_sb_b, SB_HEADS, 4 * HEAD_DIM, HEAD_DIM, lambda h: (h // 2) * HEAD_DIM)
    w_nsa_s = _padded_out_weight(w_nsa_b, NSA_HEADS, LANES, HEAD_DIM, lambda h: (h // NSA_REP) * HEAD_DIM)
    w_diff_s = _padded_out_weight(w_diff_b, 2 * DIFF_HEADS, 2 * DIFF_VDIM, DIFF_VDIM, lambda h: (h // 2) * DIFF_VDIM)

    rope_p = _rope_tables(jnp.arange(S, dtype=jnp.int32))
    rope_s = _rope_tables(jnp.full((NB_S,), past, dtype=jnp.int32))
    cache_sb4 = cache_sb_kv.reshape(depth, cache_sb_kv.shape[1], page, -1)
    cache_nsa4 = cache_nsa_kv.reshape(depth, cache_nsa_kv.shape[1], page, -1)
    cache_diff4 = cache_diff_kv.reshape(depth, cache_diff_kv.shape[1], page, -1)
    wlen = state_nsa_win.shape[2]
    win_state4 = state_nsa_win.reshape(depth, NB_S, wlen, -1)

    xp = x_prompt.reshape(T, D)
    xs = x_sample.reshape(NB_S, D)
    yp = ys = None
    outs = {k: [] for k in ("sb_p", "sb_s", "nsa_p", "nsa_s", "diff_p", "diff_s", "win_p", "win_s", "conv_p", "conv_s")}
    for l in range(depth):
        lam_init = 0.8 - 0.6 * math.exp(-0.3 * l)
        consts = jnp.full((HALO, LANES), lam_init, F32)
        g_attn = ln_attn[l].reshape(1, D)
        g_ffn = ln_ffn[l].reshape(1, D)
        g_fin = ln_final.reshape(1, D)
        subln = diff_subln[l].reshape(1, DIFF_VDIM)
        pe2, wk2, wv2 = _compress_weights(nsa_cmp_pe[l], nsa_cmp_wk[l], nsa_cmp_wv[l])
        conv_b = ffn_conv_b[l].reshape(1, dff)

        (sbq, sbkv, sbkv_b, nsaq, nsakv, nsakv_b, winkv, winkv_b, nsag, dq, dkv, dkv_b, mg) = _inproj(
            xp, g_attn, w_in_b[l], rope_p, tm, S // tm)
        o_sb = _sb_prompt(sbq, sbkv_b, B, S, tq)
        o_diff = _diff_prompt(dq, dkv_b, diff_lambda[l], subln, consts, B, S, tq)
        cb = _compress_prompt(nsakv, pe2, wk2, wv2, B, S)
        o_nsa = _nsa_prompt(nsaq, nsag, cb, nsakv_b, winkv_b, B, S, tq_nsa)
        xp = _merge(xp, mg, o_sb, o_nsa, o_diff, w_sb_b[l], w_nsa_b[l], w_diff_b[l], w_o_b[l], tm)
        xp, tails, yp = _ffn_prompt(xp, g_ffn, w_up_b[l], ffn_conv_w[l], conv_b, w_down_b[l], g_fin, S, tm)
        outs["sb_p"].append(sbkv.reshape(B, S, 2, SB_KV, HEAD_DIM))
        outs["nsa_p"].append(nsakv.reshape(B, S, 4, NSA_KV, HEAD_DIM))
        outs["diff_p"].append(dkv.reshape(B, S, 2, DIFF_KV, DIFF_VDIM))
        keep = min(NSA_WINDOW, S)
        outs["win_p"].append(winkv.reshape(B, S, 2, NSA_KV, HEAD_DIM)[:, S - keep:])
        outs["conv_p"].append(tails.reshape(B, S // tm, HALO, dff)[:, -1, HALO - (CONV_W - 1):])

        (sbq, sbkv, _, nsaq, nsakv, _, winkv, _, nsag, dq, dkv, _, mg) = _inproj(
            xs, g_attn, w_in_b[l], rope_s, NB_S, 1)
        o_sb = _sb_decode(cache_sb4, l, page_table, sbq.reshape(NB_S, SB_HEADS, LANES))
        dq4 = dq.reshape(NB_S, DIFF_HEADS, LANES).astype(F32)
        lane = jnp.arange(LANES) < HEAD_DIM
        dq8 = jnp.concatenate([jnp.where(lane, dq4, 0.0), jnp.where(lane, 0.0, dq4)], axis=1).astype(BF16)
        o_diff = _diff_decode(cache_diff4, l, page_table, dq8, dkv.reshape(NB_S, 1, -1),
                              diff_lambda[l], subln, consts)
        cb = _compress_decode(cache_nsa4, l, page_table, pe2, wk2, wv2, min(4, NB_S))
        o_nsa = _nsa_decode(cache_nsa4, l, page_table, nsaq.reshape(NB_S, NSA_HEADS, LANES),
                            nsag.reshape(NB_S, 1, LANES), cb.reshape(NB_S, past // NSA_BLOCK, 2 * LANES),
                            nsakv.reshape(NB_S, 1, -1), win_state4, winkv.reshape(NB_S, 1, -1))
        xs = _merge(xs, mg, o_sb.reshape(NB_S, -1), o_nsa.reshape(NB_S, -1), o_diff.reshape(NB_S, -1),
                    w_sb_s[l], w_nsa_s[l], w_diff_s[l], w_o_b[l], NB_S)
        prev = state_ffn_conv[l]
        xs, a_new, ys = _ffn_sample(xs, prev[:, 0], prev[:, 1], g_ffn, w_up_b[l], ffn_conv_w[l], conv_b,
                                    w_down_b[l], g_fin)
        outs["sb_s"].append(sbkv.reshape(NB_S, 1, 2, SB_KV, HEAD_DIM))
        outs["nsa_s"].append(nsakv.reshape(NB_S, 1, 4, NSA_KV, HEAD_DIM))
        outs["diff_s"].append(dkv.reshape(NB_S, 1, 2, DIFF_KV, DIFF_VDIM))
        win_full = jnp.concatenate([state_nsa_win[l], winkv.reshape(NB_S, 1, 2, NSA_KV, HEAD_DIM)], axis=1)
        outs["win_s"].append(win_full[:, win_full.shape[1] - wlen:])
        outs["conv_s"].append(jnp.concatenate([prev[:, 1:], a_new[:, None]], axis=1))

    st = lambda k: jnp.stack(outs[k])
    return (yp.reshape(B, S, D), ys.reshape(NB_S, 1, D),
            st("sb_p"), st("sb_s"), st("nsa_p"), st("nsa_s"), st("diff_p"), st("diff_s"),
            st("win_p"), st("win_s"), st("conv_p"), st("conv_s"))
```

```python
import functools
import math

import jax
import jax.numpy as jnp
from jax import lax
from jax.experimental import pallas as pl
from jax.experimental.pallas import tpu as pltpu

F32 = jnp.float32
BF16 = jnp.bfloat16

HEAD_DIM = 64
SB_HEADS, SB_KV = 8, 4
NSA_HEADS, NSA_KV = 8, 2
NSA_REP = NSA_HEADS // NSA_KV
NSA_BLOCK = 64
NSA_SEL = 16
NSA_WINDOW = 512
DIFF_HEADS, DIFF_KV = 4, 2
DIFF_VDIM = 2 * HEAD_DIM
ROPE_THETA = 500000.0
ROT_DIM = HEAD_DIM // 4
CONV_W = 3
EPS = 1e-6
NEG_INF = -1e30
SCALE = HEAD_DIM ** -0.5

LANES = 128
HALO = 8

C_SBQ = (0, 512)
C_SBKV = (512, 1024)
C_NSAQ = (1024, 1536)
C_NSAKV = (1536, 2048)
C_WIN = (2048, 2304)
C_NSAG = (2304, 2432)
C_DQ = (2432, 2944)
C_DKV = (2944, 3456)
C_MG = 3456
N_GATE_RAW = 3 * NSA_HEADS

VMEM_LIMIT = 56 * 1024 * 1024


def _cparams(sem):
    return pltpu.CompilerParams(dimension_semantics=sem, vmem_limit_bytes=VMEM_LIMIT)


def _dot(a, b):
    return jnp.dot(a, b, preferred_element_type=F32)


def _dot_nt(a, b):
    return lax.dot_general(a, b, (((1,), (1,)), ((), ())), preferred_element_type=F32)


def _rms(x, g):
    ms = jnp.mean(x * x, axis=-1, keepdims=True)
    return x * lax.rsqrt(ms + EPS) * g


def _sigmoid(x):
    return 1.0 / (1.0 + jnp.exp(-x))


def _log_keep(z):
    return -(jnp.maximum(z, 0.0) + jnp.log(1.0 + jnp.exp(-jnp.abs(z))))


def _resident(shape, index_map):
    return pl.BlockSpec(shape, index_map, pipeline_mode=pl.Buffered(1))


def _inproj_kernel(x_ref, g_ref, w_ref, cos_ref, s1_ref, s2_ref,
                   sbq_ref, sbkv_ref, sbkvb_ref, nsaq_ref, nsakv_ref, nsakvb_ref,
                   winkv_ref, winkvb_ref, nsag_ref, dq_ref, dkv_ref, dkvb_ref, mg_ref):
    tm = x_ref.shape[0]
    hb = _rms(x_ref[...], g_ref[...]).astype(BF16)
    cos, s1, s2 = cos_ref[...], s1_ref[...], s2_ref[...]
    lo_half = lax.broadcasted_iota(jnp.int32, (tm, LANES), 1) < HEAD_DIM

    def mm(c):
        return _dot(hb, w_ref[:, c[0]:c[1]])

    def rope(u):
        return u * cos + pltpu.roll(u, LANES - ROT_DIM // 2, 1) * s1 + pltpu.roll(u, ROT_DIM // 2, 1) * s2

    def tiles(u):
        return [u[:, t * LANES:(t + 1) * LANES] for t in range(u.shape[1] // LANES)]

    def padded_heads(u, half_of_head, do_rope):
        out = []
        for t, ut in enumerate(tiles(u)):
            if do_rope:
                ut = rope(ut)
            ut = ut * SCALE
            ur = pltpu.roll(ut, HEAD_DIM, 1)
            for hh in range(2):
                tgt = half_of_head(2 * t + hh)
                src = ut if tgt == hh else ur
                out.append(jnp.where(lo_half if tgt == 0 else jnp.logical_not(lo_half), src, 0.0))
        return jnp.concatenate(out, axis=1).astype(BF16)

    sbq_ref[...] = padded_heads(mm(C_SBQ), lambda h: (h // 2) % 2, False)
    u = mm(C_SBKV)
    sbkv_ref[...] = u
    sbkvb_ref[...] = u.astype(BF16)
    nsaq_ref[...] = padded_heads(mm(C_NSAQ), lambda h: h // NSA_REP, True)
    t = tiles(mm(C_NSAKV))
    u = jnp.concatenate([rope(t[0]), t[1], rope(t[2]), t[3]], axis=1)
    nsakv_ref[...] = u
    nsakvb_ref[...] = u.astype(BF16)
    t = tiles(mm(C_WIN))
    u = jnp.concatenate([rope(t[0]), t[1]], axis=1)
    winkv_ref[...] = u
    winkvb_ref[...] = u.astype(BF16)
    nsag_ref[...] = _sigmoid(mm(C_NSAG))
    dq_ref[...] = (jnp.concatenate([rope(ut) for ut in tiles(mm(C_DQ))], axis=1) * SCALE).astype(BF16)
    t = tiles(mm(C_DKV))
    u = jnp.concatenate([rope(t[0]), rope(t[1]), t[2], t[3]], axis=1)
    dkv_ref[...] = u
    dkvb_ref[...] = u.astype(BF16)
    d3 = mg_ref.shape[1]
    for c in range(3):
        mg_ref[:, c * (d3 // 3):(c + 1) * (d3 // 3)] = _sigmoid(
            mm((C_MG + c * (d3 // 3), C_MG + (c + 1) * (d3 // 3))))


def _inproj(x2d, g, w, rope_tabs, tm, n_pos_tiles):
    T, D = x2d.shape
    n_w = w.shape[1]
    row = lambda i: (i, 0)
    tab = lambda i: (i % n_pos_tiles, 0)
    widths = [(1024, BF16), (512, F32), (512, BF16), (1024, BF16), (512, F32), (512, BF16),
              (256, F32), (256, BF16), (LANES, F32), (512, BF16), (512, F32), (512, BF16), (3 * D, F32)]
    return pl.pallas_call(
        _inproj_kernel,
        grid=(T // tm,),
        in_specs=[pl.BlockSpec((tm, D), row), _resident((1, D), lambda i: (0, 0)),
                  _resident((D, n_w), lambda i: (0, 0)),
                  pl.BlockSpec((tm, LANES), tab), pl.BlockSpec((tm, LANES), tab), pl.BlockSpec((tm, LANES), tab)],
        out_specs=[pl.BlockSpec((tm, wd), row) for wd, _ in widths],
        out_shape=[jax.ShapeDtypeStruct((T, wd), dt) for wd, dt in widths],
        compiler_params=_cparams(("parallel",)),
        name="inproj",
    )(x2d, g, w, *rope_tabs)


def _rope_tables(pos):
    half = ROT_DIM // 2
    inv_freq = ROPE_THETA ** (-jnp.arange(half, dtype=F32) * 2.0 / ROT_DIM)
    ang = pos.astype(F32)[:, None] * inv_freq
    cos, sin = jnp.cos(ang), jnp.sin(ang)
    lane = jnp.arange(LANES) % HEAD_DIM
    idx = lane % half
    c = jnp.where(lane < ROT_DIM, cos[:, idx], 1.0)
    s1 = jnp.where(lane < half, -sin[:, idx], 0.0)
    s2 = jnp.where((lane >= half) & (lane < ROT_DIM), sin[:, idx], 0.0)
    return c, s1, s2


def _sb_kernel(q_ref, k_ref, v_ref, o_ref):
    tq = q_ref.shape[0]
    tk = tq
    i = pl.program_id(2)
    n_heads = q_ref.shape[1] // LANES
    rows = n_heads * tq
    q2 = jnp.concatenate([q_ref[:, h * LANES:(h + 1) * LANES] for h in range(n_heads)], axis=0)
    tri = (lax.broadcasted_iota(jnp.int32, (tk, tk), 0) > lax.broadcasted_iota(jnp.int32, (tk, tk), 1)).astype(BF16)

    def block(j, carry, diagonal):
        acc, run = carry
        k = k_ref[pl.ds(pl.multiple_of(j * tk, tk), tk), :]
        v = v_ref[pl.ds(pl.multiple_of(j * tk, tk), tk), :]
        z = _dot_nt(q2, k)
        lk = _log_keep(z)
        if diagonal:
            qrow = lax.broadcasted_iota(jnp.int32, (rows, tk), 0) % tq
            mask = lax.broadcasted_iota(jnp.int32, (rows, tk), 1) < qrow
            lk = jnp.where(mask, lk, 0.0)
        hi = lk.astype(BF16)
        lo = (lk - hi.astype(F32)).astype(BF16)
        between = _dot(hi, tri) + _dot(lo, tri)
        e = jnp.exp(lk + z + between + run)
        if diagonal:
            e = jnp.where(mask, e, 0.0)
        acc = acc + _dot(e.astype(BF16), v)
        run = run + jnp.sum(lk, axis=1, keepdims=True)
        return acc, run

    carry = block(i, (jnp.zeros((rows, LANES), F32), jnp.zeros((rows, 1), F32)), True)
    acc, _ = lax.fori_loop(0, i, lambda t, c: block(i - 1 - t, c, False), carry)
    a = [acc[h * tq:(h + 1) * tq] for h in range(n_heads)]
    lo_half = lax.broadcasted_iota(jnp.int32, (tq, LANES), 1) < HEAD_DIM
    first = jnp.where(lo_half, a[0], pltpu.roll(a[1], HEAD_DIM, 1))
    second = jnp.where(lo_half, pltpu.roll(a[2], HEAD_DIM, 1), a[3])
    o_ref[...] = jnp.concatenate([first, second], axis=1).astype(o_ref.dtype)


def _sb_prompt(q_pad, kv_b, B, S, tq):
    T = B * S
    nq = S // tq
    n_pairs = SB_KV // 2
    return pl.pallas_call(
        _sb_kernel,
        grid=(B, n_pairs, nq),
        in_specs=[pl.BlockSpec((tq, 4 * LANES), lambda b, p, i: (b * nq + i, p)),
                  pl.BlockSpec((S, LANES), lambda b, p, i: (b, p)),
                  pl.BlockSpec((S, LANES), lambda b, p, i: (b, n_pairs + p))],
        out_specs=pl.BlockSpec((tq, 2 * LANES), lambda b, p, i: (b * nq + i, p)),
        out_shape=jax.ShapeDtypeStruct((T, SB_HEADS * HEAD_DIM), BF16),
        compiler_params=_cparams(("parallel", "parallel", "arbitrary")),
        name="sb_prompt",
    )(q_pad, kv_b, kv_b)


def _diff_lambda(lamv_ref, c_ref):
    lv = lamv_ref[...]
    a = jnp.sum(lv[0:1] * lv[1:2], axis=1, keepdims=True)
    b = jnp.sum(lv[2:3] * lv[3:4], axis=1, keepdims=True)
    lam_init = c_ref[0:1, 0:1]
    return jnp.exp(a) - jnp.exp(b) + lam_init, lam_init


def _softmax_step(s, mask, carry, v):
    m, l, acc = carry
    if mask is not None:
        s = jnp.where(mask, s, NEG_INF)
    m_new = jnp.maximum(m, jnp.max(s, axis=1, keepdims=True))
    p = jnp.exp(s - m_new)
    if mask is not None:
        p = jnp.where(mask, p, 0.0)
    alpha = jnp.exp(m - m_new)
    l = alpha * l + jnp.sum(p, axis=1, keepdims=True)
    acc = alpha * acc + _dot(p.astype(BF16), v)
    return m_new, l, acc


def _softmax_init(rows, width):
    return (jnp.full((rows, 1), NEG_INF, F32), jnp.zeros((rows, 1), F32), jnp.zeros((rows, width), F32))


def _diff_kernel(q_ref, k_ref, v_ref, lamv_ref, subln_ref, c_ref, o_ref):
    tq = q_ref.shape[0]
    tk = tq
    i = pl.program_id(2)
    rows = 4 * tq
    lo_half = lax.broadcasted_iota(jnp.int32, (tq, LANES), 1) < HEAD_DIM
    qs = []
    for m in range(2):
        for h in range(2):
            qh = q_ref[:, h * LANES:(h + 1) * LANES].astype(F32)
            qs.append(jnp.where(lo_half if m == 0 else jnp.logical_not(lo_half), qh, 0.0).astype(BF16))
    q4 = jnp.concatenate(qs, axis=0)

    def block(j, carry, diagonal):
        k = k_ref[pl.ds(pl.multiple_of(j * tk, tk), tk), :]
        v = v_ref[pl.ds(pl.multiple_of(j * tk, tk), tk), :]
        s = _dot_nt(q4, k)
        mask = None
        if diagonal:
            r = lax.broadcasted_iota(jnp.int32, (rows, tk), 0)
            mask = lax.broadcasted_iota(jnp.int32, (rows, tk), 1) <= (r % tq)
        return _softmax_step(s, mask, carry, v)

    carry = block(i, _softmax_init(rows, LANES), True)
    _, l, acc = lax.fori_loop(0, i, lambda t, c: block(i - 1 - t, c, False), carry)
    a = acc / l
    lam, lam_init = _diff_lambda(lamv_ref, c_ref)
    outs = []
    for h in range(2):
        o = a[h * tq:(h + 1) * tq] - lam * a[(2 + h) * tq:(3 + h) * tq]
        outs.append(_rms(o, subln_ref[...]) * (1.0 - lam_init))
    o_ref[...] = jnp.concatenate(outs, axis=1).astype(o_ref.dtype)


def _diff_prompt(q_b, kv_b, lamv, subln, consts, B, S, tq):
    T = B * S
    nq = S // tq
    whole = lambda b, g, i: (0, 0)
    return pl.pallas_call(
        _diff_kernel,
        grid=(B, DIFF_KV, nq),
        in_specs=[pl.BlockSpec((tq, 2 * LANES), lambda b, g, i: (b * nq + i, g)),
                  pl.BlockSpec((S, LANES), lambda b, g, i: (b, g)),
                  pl.BlockSpec((S, LANES), lambda b, g, i: (b, 2 + g)),
                  pl.BlockSpec(lamv.shape, whole), pl.BlockSpec(subln.shape, whole),
                  pl.BlockSpec(consts.shape, whole)],
        out_specs=pl.BlockSpec((tq, 2 * LANES), lambda b, g, i: (b * nq + i, g)),
        out_shape=jax.ShapeDtypeStruct((T, DIFF_HEADS * DIFF_VDIM), BF16),
        compiler_params=_cparams(("parallel", "parallel", "arbitrary")),
        name="diff_prompt",
    )(q_b, kv_b, kv_b, lamv, subln, consts)


def _compress_blocks(buf_k, buf_v, pe_ref, wk_ref, wv_ref, o_ref):
    nblk = buf_k.shape[0] // NSA_BLOCK

    def body(j, acc):
        pe = pe_ref[pl.ds(j, 1), :]
        xk = (buf_k[pl.ds(j, nblk, stride=NSA_BLOCK), :] + pe).astype(BF16)
        xv = (buf_v[pl.ds(j, nblk, stride=NSA_BLOCK), :] + pe).astype(BF16)
        return acc + jnp.concatenate([_dot(xk, wk_ref[j]), _dot(xv, wv_ref[j])], axis=1)

    o_ref[...] = lax.fori_loop(0, NSA_BLOCK, body, jnp.zeros((nblk, 2 * LANES), F32)).astype(o_ref.dtype)


def _compress_prompt_kernel(x_ref, pe_ref, wk_ref, wv_ref, o_ref, buf_k, buf_v):
    buf_k[...] = x_ref[:, :LANES]
    buf_v[...] = x_ref[:, LANES:]
    _compress_blocks(buf_k, buf_v, pe_ref, wk_ref, wv_ref, o_ref)


def _compress_decode_kernel(pt_ref, pe_ref, wk_ref, wv_ref, *refs, seqs_per_block):
    *pages, o_ref, buf_k, buf_v = refs
    page = pages[0].shape[1]
    t = pl.program_id(0) % seqs_per_block
    for k, p_ref in enumerate(pages):
        x = p_ref[...].T
        dst = pl.ds(pl.multiple_of((t * len(pages) + k) * page, page), page)
        buf_k[dst, :] = x[:, :LANES]
        buf_v[dst, :] = x[:, LANES:]

    @pl.when(t == seqs_per_block - 1)
    def _():
        _compress_blocks(buf_k, buf_v, pe_ref, wk_ref, wv_ref, o_ref)


def _compress_weights(pe, wk, wv):
    eye = jnp.eye(NSA_KV, dtype=F32)
    bd = lambda w: jnp.einsum('ab,jde->jadbe', eye, w).reshape(NSA_BLOCK, LANES, LANES).astype(BF16)
    return jnp.tile(pe, (1, NSA_KV)), bd(wk), bd(wv)


def _compress_prompt(nsakv, pe2, wk2, wv2, B, S):
    chunk = min(S, 2048)
    nc = S // chunk
    whole2 = lambda b: (0, 0)
    whole3 = lambda b: (0, 0, 0)
    return pl.pallas_call(
        _compress_prompt_kernel,
        grid=(B * nc,),
        in_specs=[pl.BlockSpec((chunk, 2 * LANES), lambda b: (b, 0)),
                  pl.BlockSpec(pe2.shape, whole2), pl.BlockSpec(wk2.shape, whole3), pl.BlockSpec(wv2.shape, whole3)],
        out_specs=pl.BlockSpec((chunk // NSA_BLOCK, 2 * LANES), lambda b: (b, 0)),
        out_shape=jax.ShapeDtypeStruct((B * S // NSA_BLOCK, 2 * LANES), BF16),
        scratch_shapes=[pltpu.VMEM((chunk, LANES), F32)] * 2,
        compiler_params=_cparams(("parallel",)),
        name="compress_prompt",
    )(nsakv, pe2, wk2, wv2)


def _page_specs(npg, layer, rows, row_block):
    def spec(k):
        return pl.BlockSpec((None, None, rows, LANES), lambda b, pt: (layer, pt[b, k], row_block, 0))
    return [spec(k) for k in range(npg)]


def _compress_decode(cache_t, layer, page_table, pe2, wk2, wv2, seqs_per_block):
    nb, npg = page_table.shape
    page = cache_t.shape[3]
    sg = seqs_per_block
    rows = sg * npg * page
    whole2 = lambda b, pt: (0, 0)
    whole3 = lambda b, pt: (0, 0, 0)
    grid_spec = pltpu.PrefetchScalarGridSpec(
        num_scalar_prefetch=1,
        grid=(nb,),
        in_specs=[pl.BlockSpec(pe2.shape, whole2), pl.BlockSpec(wk2.shape, whole3), pl.BlockSpec(wv2.shape, whole3)]
        + _page_specs(npg, layer, 2 * LANES, 0),
        out_specs=pl.BlockSpec((rows // NSA_BLOCK, 2 * LANES), lambda b, pt: (b // sg, 0)),
        scratch_shapes=[pltpu.VMEM((rows, LANES), F32)] * 2,
    )
    return pl.pallas_call(
        functools.partial(_compress_decode_kernel, seqs_per_block=sg),
        grid_spec=grid_spec,
        out_shape=jax.ShapeDtypeStruct((nb * npg * page // NSA_BLOCK, 2 * LANES), BF16),
        compiler_params=_cparams(("arbitrary",)),
        name="compress_decode",
    )(page_table, pe2, wk2, wv2, *([cache_t] * npg))


def _top_blocks(imp, n_top):
    nb = imp.shape[1]
    col = lax.broadcasted_iota(jnp.int32, imp.shape, 1)

    def body(_, carry):
        vals, sel = carry
        m = jnp.max(vals, axis=1, keepdims=True)
        idx = jnp.min(jnp.where(vals == m, col, nb), axis=1, keepdims=True)
        hit = col == idx
        sel = jnp.where(jnp.logical_and(hit, m >= 0.0), 1.0, sel)
        vals = jnp.where(hit, -2.0, vals)
        return vals, sel

    _, sel = lax.fori_loop(0, n_top, body, (imp, jnp.zeros(imp.shape, F32)))
    return sel


def _compressed_branch(q, kc, vc, n_valid_of_row, cur_of_row, group_rows):
    sc = _dot_nt(q, kc)
    col = lax.broadcasted_iota(jnp.int32, sc.shape, 1)
    cmask = col < n_valid_of_row
    mx = jnp.max(jnp.where(cmask, sc, NEG_INF), axis=1, keepdims=True)
    e = jnp.where(cmask, jnp.exp(sc - mx), 0.0)
    l = jnp.sum(e, axis=1, keepdims=True)
    pc = e / jnp.where(l > 0.0, l, 1.0)
    return _dot(pc.astype(BF16), vc), pc


def _gate(gates, head, branch, rows):
    c = head * 3 + branch
    return gates[:, c:c + 1]


def _nsa_kernel(q_ref, gate_ref, cb_ref, sel_ref, win_ref, onehot_ref, o_ref, *, n_top):
    tq = q_ref.shape[0]
    nb = cb_ref.shape[0]
    tk = min(2 * LANES, sel_ref.shape[0])
    i = pl.program_id(1)
    q0 = i * tq
    j_diag = q0 // tk
    rows = NSA_HEADS * tq
    pos1 = q0 + lax.broadcasted_iota(jnp.int32, (tq, 1), 0)
    pos8 = q0 + lax.broadcasted_iota(jnp.int32, (rows, 1), 0) % tq
    kc, vc = cb_ref[:, :LANES], cb_ref[:, LANES:]
    q8 = jnp.concatenate([q_ref[:, h * LANES:(h + 1) * LANES] for h in range(NSA_HEADS)], axis=0)
    o_cmp, pc = _compressed_branch(q8, kc, vc, (pos8 + 1) // NSA_BLOCK, None, None)
    imps = []
    for g in range(NSA_KV):
        imp = pc[NSA_REP * g * tq:(NSA_REP * g + 1) * tq]
        for r in range(1, NSA_REP):
            imp = imp + pc[(NSA_REP * g + r) * tq:(NSA_REP * g + r + 1) * tq]
        imps.append(imp)
    cur2 = jnp.concatenate([pos1, pos1], axis=0) // NSA_BLOCK
    col2 = lax.broadcasted_iota(jnp.int32, (2 * tq, nb), 1)
    imp2 = jnp.where(col2 < cur2, jnp.concatenate(imps, axis=0), -1.0)
    sel2 = jnp.where(col2 == cur2, 1.0, _top_blocks(imp2, n_top))

    gates = gate_ref[...]
    lo_half = lax.broadcasted_iota(jnp.int32, (tq, LANES), 1) < HEAD_DIM
    n_sel_blocks = (q0 + tq - 1) // tk + 1
    w_lo = jnp.maximum(q0 - NSA_WINDOW, 0) // tk
    m8 = jnp.concatenate([sel2[g * tq:(g + 1) * tq] for g in range(NSA_KV) for _ in range(NSA_REP)], axis=0)
    qx = jnp.concatenate([q8, ((m8 - 1.0) * -NEG_INF).astype(BF16)], axis=1)

    def sel_scores(j):
        k0 = pl.multiple_of(j * tk, tk)
        kx = jnp.concatenate([sel_ref[pl.ds(k0, tk), :LANES], onehot_ref[pl.ds(k0, tk), :]], axis=1)
        return _dot_nt(qx, kx), sel_ref[pl.ds(k0, tk), LANES:]

    s, v = sel_scores(j_diag)
    kpos = j_diag * tk + lax.broadcasted_iota(jnp.int32, (rows, tk), 1)
    carry = _softmax_step(jnp.where(kpos <= pos8, s, NEG_INF), None, _softmax_init(rows, LANES), v)

    def sel_block(j, carry):
        s, v = sel_scores(j)
        return _softmax_step(s, None, carry, v)

    _, l, acc = lax.fori_loop(0, j_diag, sel_block, carry)
    o_sel = acc / l

    def win_block(j, carry):
        k0 = pl.multiple_of(j * tk, tk)
        k = win_ref[pl.ds(k0, tk), :LANES]
        v = win_ref[pl.ds(k0, tk), LANES:]
        s = _dot_nt(q8, k)
        kpos = k0 + lax.broadcasted_iota(jnp.int32, (rows, tk), 1)
        return _softmax_step(s, jnp.logical_and(kpos <= pos8, kpos >= pos8 - NSA_WINDOW), carry, v)

    _, l, acc = lax.fori_loop(w_lo, n_sel_blocks, win_block, _softmax_init(rows, LANES))
    o_win = acc / l

    heads = []
    for h in range(NSA_HEADS):
        sl = slice(h * tq, (h + 1) * tq)
        heads.append(_gate(gates, h, 0, tq) * o_cmp[sl] + _gate(gates, h, 1, tq) * o_sel[sl]
                     + _gate(gates, h, 2, tq) * o_win[sl])
    out_tiles = []
    for t in range(NSA_HEADS // 2):
        a, b = heads[2 * t], heads[2 * t + 1]
        if (2 * t) // NSA_REP == 0:
            out_tiles.append(jnp.where(lo_half, a, pltpu.roll(b, HEAD_DIM, 1)))
        else:
            out_tiles.append(jnp.where(lo_half, pltpu.roll(a, HEAD_DIM, 1), b))
    o_ref[...] = jnp.concatenate(out_tiles, axis=1).astype(o_ref.dtype)


def _nsa_prompt(q_pad, gates, cb, nsakv_b, winkv_b, B, S, tq):
    T = B * S
    nq = S // tq
    nb = S // NSA_BLOCK
    onehot = (jnp.arange(S)[:, None] // NSA_BLOCK == jnp.arange(nb)[None, :]).astype(BF16)
    return pl.pallas_call(
        functools.partial(_nsa_kernel, n_top=min(NSA_SEL - 1, nb)),
        grid=(B, nq),
        in_specs=[pl.BlockSpec((tq, NSA_HEADS * LANES), lambda b, i: (b * nq + i, 0)),
                  pl.BlockSpec((tq, LANES), lambda b, i: (b * nq + i, 0)),
                  pl.BlockSpec((nb, 2 * LANES), lambda b, i: (b, 0)),
                  pl.BlockSpec((S, 2 * LANES), lambda b, i: (b, 1)),
                  pl.BlockSpec((S, 2 * LANES), lambda b, i: (b, 0)),
                  _resident((S, nb), lambda b, i: (0, 0))],
        out_specs=pl.BlockSpec((tq, NSA_HEADS * HEAD_DIM), lambda b, i: (b * nq + i, 0)),
        out_shape=jax.ShapeDtypeStruct((T, NSA_HEADS * HEAD_DIM), BF16),
        compiler_params=_cparams(("parallel", "arbitrary")),
        name="nsa_prompt",
    )(q_pad, gates, cb, nsakv_b, winkv_b, onehot)


def _merge_kernel(x_ref, mg_ref, osb_ref, onsa_ref, odiff_ref, wsb_ref, wnsa_ref, wdiff_ref, wo_ref, y_ref):
    d = x_ref.shape[1]
    merged = (mg_ref[:, 0:d] * _dot(osb_ref[...].astype(BF16), wsb_ref[...])
              + mg_ref[:, d:2 * d] * _dot(onsa_ref[...].astype(BF16), wnsa_ref[...])
              + mg_ref[:, 2 * d:3 * d] * _dot(odiff_ref[...].astype(BF16), wdiff_ref[...]))
    y_ref[...] = x_ref[...] + _dot(merged.astype(BF16), wo_ref[...])


def _merge(x2d, mg, o_sb, o_nsa, o_diff, w_sb, w_nsa, w_diff, w_o, tm):
    T, D = x2d.shape
    row = lambda i: (i, 0)
    whole = lambda i: (0, 0)
    return pl.pallas_call(
        _merge_kernel,
        grid=(T // tm,),
        in_specs=[pl.BlockSpec((tm, D), row), pl.BlockSpec((tm, 3 * D), row),
                  pl.BlockSpec((tm, o_sb.shape[1]), row), pl.BlockSpec((tm, o_nsa.shape[1]), row),
                  pl.BlockSpec((tm, o_diff.shape[1]), row),
                  _resident(w_sb.shape, whole), _resident(w_nsa.shape, whole),
                  _resident(w_diff.shape, whole), _resident(w_o.shape, whole)],
        out_specs=pl.BlockSpec((tm, D), row),
        out_shape=jax.ShapeDtypeStruct((T, D), F32),
        compiler_params=_cparams(("parallel",)),
        name="merge",
    )(x2d, mg, o_sb, o_nsa, o_diff, w_sb, w_nsa, w_diff, w_o)


def _gelu(c):
    return 0.5 * c * (1.0 + jnp.tanh(math.sqrt(2.0 / math.pi) * (c + 0.044715 * (c * c * c))))


def _ffn_prompt_kernel(x_ref, halo_ref, g_ref, wup_ref, cw_ref, cb_ref, wdown_ref, gf_ref,
                       y_ref, tail_ref, yn_ref, a_scr, *, tiles_per_seq):
    tm = x_ref.shape[0]
    dff = cw_ref.shape[1]
    i = pl.program_id(0)
    x = x_ref[...]
    hb = _rms(x, g_ref[...]).astype(BF16)
    a = _dot(hb, wup_ref[:, :dff])
    gate = _dot(hb, wup_ref[:, dff:])
    a_halo = _dot(_rms(halo_ref[...], g_ref[...]).astype(BF16), wup_ref[:, :dff])
    a_scr[0:HALO, :] = jnp.where(i % tiles_per_seq == 0, 0.0, a_halo)
    a_scr[HALO:HALO + tm, :] = a
    c = cb_ref[...] + a * cw_ref[CONV_W - 1:CONV_W, :]
    for j in range(CONV_W - 1):
        off = HALO - (CONV_W - 1) + j
        c = c + a_scr[off:off + tm, :] * cw_ref[j:j + 1, :]
    y = x + _dot((_gelu(c) * gate).astype(BF16), wdown_ref[...])
    y_ref[...] = y
    yn_ref[...] = _rms(y, gf_ref[...])
    tail_ref[...] = a_scr[tm:tm + HALO, :]


def _ffn_prompt(x2d, g, w_up, conv_w, conv_b, w_down, g_final, S, tm):
    T, D = x2d.shape
    dff = conv_w.shape[1]
    whole = lambda i: (0, 0)
    hb = tm // HALO
    return pl.pallas_call(
        functools.partial(_ffn_prompt_kernel, tiles_per_seq=S // tm),
        grid=(T // tm,),
        in_specs=[pl.BlockSpec((tm, D), lambda i: (i, 0)),
                  pl.BlockSpec((HALO, D), lambda i: (jnp.maximum(i * hb - 1, 0), 0)),
                  _resident((1, D), whole), _resident(w_up.shape, whole), _resident(conv_w.shape, whole),
                  _resident(conv_b.shape, whole), _resident(w_down.shape, whole), _resident((1, D), whole)],
        out_specs=[pl.BlockSpec((tm, D), lambda i: (i, 0)),
                   pl.BlockSpec((None, HALO, dff), lambda i: (i, 0, 0)),
                   pl.BlockSpec((tm, D), lambda i: (i, 0))],
        out_shape=[jax.ShapeDtypeStruct((T, D), F32), jax.ShapeDtypeStruct((T // tm, HALO, dff), F32),
                   jax.ShapeDtypeStruct((T, D), F32)],
        scratch_shapes=[pltpu.VMEM((tm + HALO, dff), F32)],
        compiler_params=_cparams(("parallel",)),
        name="ffn_prompt",
    )(x2d, x2d, g, w_up, conv_w, conv_b, w_down, g_final)


def _ffn_sample_kernel(x_ref, p0_ref, p1_ref, g_ref, wup_ref, cw_ref, cb_ref, wdown_ref, gf_ref,
                       y_ref, a_ref, yn_ref):
    dff = cw_ref.shape[1]
    x = x_ref[...]
    hb = _rms(x, g_ref[...]).astype(BF16)
    a = _dot(hb, wup_ref[:, :dff])
    gate = _dot(hb, wup_ref[:, dff:])
    c = cb_ref[...] + p0_ref[...] * cw_ref[0:1, :] + p1_ref[...] * cw_ref[1:2, :] + a * cw_ref[2:3, :]
    y = x + _dot((_gelu(c) * gate).astype(BF16), wdown_ref[...])
    y_ref[...] = y
    a_ref[...] = a
    yn_ref[...] = _rms(y, gf_ref[...])


def _ffn_sample(x2d, prev0, prev1, g, w_up, conv_w, conv_b, w_down, g_final):
    T, D = x2d.shape
    dff = conv_w.shape[1]
    return pl.pallas_call(
        _ffn_sample_kernel,
        out_shape=[jax.ShapeDtypeStruct((T, D), F32), jax.ShapeDtypeStruct((T, dff), F32),
                   jax.ShapeDtypeStruct((T, D), F32)],
        compiler_params=pltpu.CompilerParams(vmem_limit_bytes=VMEM_LIMIT),
        name="ffn_sample",
    )(x2d, prev0, prev1, g, w_up, conv_w, conv_b, w_down, g_final)


def _head_rows(q_ref, split):
    q8 = q_ref[...].astype(F32)
    row = lax.broadcasted_iota(jnp.int32, q8.shape, 0)
    return jnp.concatenate([jnp.where(row < split, q8, 0.0), jnp.where(row >= split, q8, 0.0)], axis=1)


def _lane_cat(pages, r0, r1):
    return jnp.concatenate([p[r0:r1, :] for p in pages], axis=1).astype(BF16)


def _sb_decode_kernel(pt_ref, q_ref, *refs):
    *pages, o_ref = refs
    npg = len(pages)
    half = pages[0].shape[0] // 2
    tk = pages[0].shape[1]
    nh = q_ref.shape[0]
    q = _head_rows(q_ref, SB_HEADS // 2).astype(BF16)
    z = _dot(q, _lane_cat(pages, 0, half))
    zs = jnp.concatenate([z[:, k * tk:(k + 1) * tk] for k in range(npg)], axis=0)
    lk = _log_keep(zs)
    tri = (lax.broadcasted_iota(jnp.int32, (tk, tk), 0) > lax.broadcasted_iota(jnp.int32, (tk, tk), 1)).astype(BF16)
    hi = lk.astype(BF16)
    lo = (lk - hi.astype(F32)).astype(BF16)
    between = _dot(hi, tri) + _dot(lo, tri)
    tot = jnp.sum(lk, axis=1, keepdims=True)
    runs, run = [], jnp.zeros((nh, 1), F32)
    for k in reversed(range(npg)):
        runs.append(run)
        run = run + tot[k * nh:(k + 1) * nh]
    e = jnp.exp(lk + zs + between + jnp.concatenate(runs[::-1], axis=0))
    e = jnp.concatenate([e[k * nh:(k + 1) * nh] for k in range(npg)], axis=1).astype(BF16)
    acc = _dot_nt(e, _lane_cat(pages, half, 2 * half))
    row = lax.broadcasted_iota(jnp.int32, acc.shape, 0)
    lane = lax.broadcasted_iota(jnp.int32, acc.shape, 1)
    o_ref[...] = jnp.where(lane // HEAD_DIM == row // 2, acc, 0.0)


def _sb_decode(cache_t, layer, page_table, q_pad3):
    nb, npg = page_table.shape
    feat = cache_t.shape[2]
    grid_spec = pltpu.PrefetchScalarGridSpec(
        num_scalar_prefetch=1,
        grid=(nb,),
        in_specs=[pl.BlockSpec((None, SB_HEADS, LANES), lambda b, pt: (b, 0, 0))] + _page_specs(npg, layer, feat, 0),
        out_specs=pl.BlockSpec((None, SB_HEADS, feat // 2), lambda b, pt: (b, 0, 0)),
    )
    return pl.pallas_call(
        _sb_decode_kernel,
        grid_spec=grid_spec,
        out_shape=jax.ShapeDtypeStruct((nb, SB_HEADS, feat // 2), F32),
        compiler_params=_cparams(("parallel",)),
        name="sb_decode",
    )(page_table, q_pad3, *([cache_t] * npg))


def _softmax_with_new_key(s, s_new):
    m = jnp.maximum(jnp.max(s, axis=1, keepdims=True), s_new)
    p = jnp.exp(s - m)
    p_new = jnp.exp(s_new - m)
    return p.astype(BF16), p_new.astype(BF16).astype(F32), jnp.sum(p, axis=1, keepdims=True) + p_new


def _diff_decode_kernel(pt_ref, q_ref, new_ref, lamv_ref, subln_ref, c_ref, *refs):
    *pages, o_ref = refs
    page = pages[0].shape[0] // 4
    half = new_ref.shape[1] // 2
    row = lax.broadcasted_iota(jnp.int32, (2 * DIFF_HEADS, LANES), 0)
    q8 = q_ref[...].astype(F32)
    qf = jnp.concatenate([jnp.where((row % DIFF_HEADS) < 2, q8, 0.0), jnp.where((row % DIFF_HEADS) >= 2, q8, 0.0)], axis=1)

    def rows_of(slot):
        return jnp.concatenate(
            [jnp.concatenate([p[pl.ds(2 * slot + g, page, stride=4), :] for g in range(DIFF_KV)], axis=1)
             for p in pages], axis=0).astype(BF16)

    s = _dot_nt(qf.astype(BF16), rows_of(0))
    k_new = new_ref[:, :half].astype(BF16).astype(F32)
    v_new = new_ref[:, half:].astype(BF16).astype(F32)
    p, p_new, l = _softmax_with_new_key(s, jnp.sum(qf * k_new, axis=1, keepdims=True))
    a = (_dot(p, rows_of(1)) + p_new * v_new) / l
    lam, lam_init = _diff_lambda(lamv_ref, c_ref)
    o = a - lam * pltpu.roll(a, DIFF_HEADS, 0)
    r2 = lax.broadcasted_iota(jnp.int32, o.shape, 0)
    lane = lax.broadcasted_iota(jnp.int32, o.shape, 1)
    valid = jnp.logical_and(r2 < DIFF_HEADS, lane // DIFF_VDIM == r2 // 2)
    o = jnp.where(valid, o, 0.0)
    ms = jnp.sum(o * o, axis=1, keepdims=True) * (1.0 / DIFF_VDIM)
    g2 = jnp.concatenate([subln_ref[...], subln_ref[...]], axis=1)
    o_ref[...] = o * lax.rsqrt(ms + EPS) * g2 * (1.0 - lam_init)


def _diff_decode(cache_r, layer, page_table, q8, new_rows3, lamv, subln, consts):
    nb, npg = page_table.shape
    rows, width = cache_r.shape[2], new_rows3.shape[2]
    whole = lambda b, pt: (0, 0)
    grid_spec = pltpu.PrefetchScalarGridSpec(
        num_scalar_prefetch=1,
        grid=(nb,),
        in_specs=[pl.BlockSpec((None, 2 * DIFF_HEADS, LANES), lambda b, pt: (b, 0, 0)),
                  pl.BlockSpec((None, 1, width), lambda b, pt: (b, 0, 0)),
                  pl.BlockSpec(lamv.shape, whole), pl.BlockSpec(subln.shape, whole), pl.BlockSpec(consts.shape, whole)]
        + _page_specs(npg, layer, rows, 0),
        out_specs=pl.BlockSpec((None, 2 * DIFF_HEADS, width // 2), lambda b, pt: (b, 0, 0)),
    )
    return pl.pallas_call(
        _diff_decode_kernel,
        grid_spec=grid_spec,
        out_shape=jax.ShapeDtypeStruct((nb, 2 * DIFF_HEADS, width // 2), F32),
        compiler_params=_cparams(("parallel",)),
        name="diff_decode",
    )(page_table, q8, new_rows3, lamv, subln, consts, *([cache_r] * npg))


def _nsa_decode_kernel(pt_ref, q_ref, gate_ref, cb_ref, new_ref, win_ref, wnew_ref, *refs, n_top, n_cmp):
    *pages, o_ref = refs
    tk = pages[0].shape[1]
    past = len(pages) * tk
    rows = NSA_HEADS
    q = q_ref[...]
    qf = q.astype(F32)
    row1 = lax.broadcasted_iota(jnp.int32, (rows, 1), 0)
    bf = lambda x: x.astype(BF16).astype(F32)

    pad = jnp.zeros((LANES - n_cmp, 2 * LANES), BF16)
    cb = jnp.concatenate([cb_ref[...], pad], axis=0)
    o_cmp, pc = _compressed_branch(q, cb[:, :LANES], cb[:, LANES:], n_cmp, None, None)
    rowf = lax.broadcasted_iota(jnp.int32, pc.shape, 0)
    imp0 = jnp.sum(jnp.where(rowf < NSA_REP, pc, 0.0), axis=0, keepdims=True)
    imp1 = jnp.sum(jnp.where(rowf >= NSA_REP, pc, 0.0), axis=0, keepdims=True)
    col = lax.broadcasted_iota(jnp.int32, pc.shape, 1)
    imp = jnp.where(col < n_cmp, jnp.where(rowf < NSA_REP, imp0, imp1), -1.0)
    sel = _top_blocks(imp, n_top)

    blk = lax.broadcasted_iota(jnp.int32, (LANES, past), 1) // NSA_BLOCK
    onehot = (lax.broadcasted_iota(jnp.int32, (LANES, past), 0) == blk).astype(BF16)
    qx = jnp.concatenate([q, ((sel - 1.0) * -NEG_INF).astype(BF16)], axis=1)
    s = _dot(qx, jnp.concatenate([_lane_cat(pages, 0, LANES), onehot], axis=0))
    s_new = jnp.sum(qf * bf(new_ref[:, 2 * LANES:3 * LANES]), axis=1, keepdims=True)
    p, p_new, l = _softmax_with_new_key(s, s_new)
    o_sel = (_dot_nt(p, _lane_cat(pages, LANES, 2 * LANES)) + p_new * bf(new_ref[:, 3 * LANES:])) / l

    s = _dot(q, win_ref[:LANES, :].astype(BF16))
    s_new = jnp.sum(qf * bf(wnew_ref[:, :LANES]), axis=1, keepdims=True)
    p, p_new, l = _softmax_with_new_key(s, s_new)
    o_win = (_dot_nt(p, win_ref[LANES:, :].astype(BF16)) + p_new * bf(wnew_ref[:, LANES:])) / l

    gates = jnp.broadcast_to(gate_ref[...], (rows, LANES))
    lane = lax.broadcasted_iota(jnp.int32, (rows, LANES), 1)
    gsel = lambda c: jnp.sum(jnp.where(lane == 3 * row1 + c, gates, 0.0), axis=1, keepdims=True)
    o = gsel(0) * o_cmp + gsel(1) * o_sel + gsel(2) * o_win
    o_ref[...] = jnp.where(lane // HEAD_DIM == row1 // NSA_REP, o, 0.0)


def _nsa_decode(cache_t, layer, page_table, q8, gates3, cb3, new_rows3, win_state_t, wnew3):
    nb, npg = page_table.shape
    n_cmp = cb3.shape[1]
    wlen = win_state_t.shape[3]
    n_top = min(NSA_SEL - 1, n_cmp + 1)
    grid_spec = pltpu.PrefetchScalarGridSpec(
        num_scalar_prefetch=1,
        grid=(nb,),
        in_specs=[pl.BlockSpec((None, NSA_HEADS, LANES), lambda b, pt: (b, 0, 0)),
                  pl.BlockSpec((None, 1, LANES), lambda b, pt: (b, 0, 0)),
                  pl.BlockSpec((None, n_cmp, 2 * LANES), lambda b, pt: (b, 0, 0)),
                  pl.BlockSpec((None, 1, 4 * LANES), lambda b, pt: (b, 0, 0)),
                  pl.BlockSpec((None, None, 2 * LANES, wlen), lambda b, pt: (layer, b, 0, 0)),
                  pl.BlockSpec((None, 1, 2 * LANES), lambda b, pt: (b, 0, 0))]
        + _page_specs(npg, layer, 2 * LANES, 1),
        out_specs=pl.BlockSpec((None, NSA_HEADS, LANES), lambda b, pt: (b, 0, 0)),
    )
    return pl.pallas_call(
        functools.partial(_nsa_decode_kernel, n_top=n_top, n_cmp=n_cmp),
        grid_spec=grid_spec,
        out_shape=jax.ShapeDtypeStruct((nb, NSA_HEADS, LANES), F32),
        compiler_params=_cparams(("parallel",)),
        name="nsa_decode",
    )(page_table, q8, gates3, cb3, new_rows3, win_state_t, wnew3, *([cache_t] * npg))


def _prep_w_in(w_in):
    g0 = C_NSAG[0]
    gate = w_in[..., g0:g0 + N_GATE_RAW]
    gate = jnp.pad(gate, ((0, 0), (0, 0), (0, LANES - N_GATE_RAW)))
    return jnp.concatenate([w_in[..., :g0], gate, w_in[..., g0 + N_GATE_RAW:]], axis=-1).astype(BF16)


def _padded_out_weight(w, n_rows, row_width, head_width, lane_of_head):
    depth, _, d = w.shape
    out = jnp.zeros((depth, n_rows * row_width, d), w.dtype)
    for h in range(w.shape[1] // head_width):
        start = h * row_width + lane_of_head(h)
        out = out.at[:, start:start + head_width].set(w[:, h * head_width:(h + 1) * head_width])
    return out


def kernel(x_prompt, x_sample, cache_sb_kv, cache_nsa_kv, cache_diff_kv, state_nsa_win, state_ffn_conv, page_table, ln_attn, w_in, nsa_cmp_pe, nsa_cmp_wk, nsa_cmp_wv, diff_lambda, diff_subln, w_sb_out, w_nsa_out, w_diff_out, w_o, ln_ffn, w_ff_up, ffn_conv_w, ffn_conv_b, w_ff_down, ln_final):
    B, S, D = x_prompt.shape
    NB_S = x_sample.shape[0]
    depth = w_in.shape[0]
    npg = page_table.shape[1]
    page = cache_sb_kv.shape[2]
    past = npg * page
    dff = ffn_conv_w.shape[2]
    T = B * S
    tq = min(256, S)
    tq_nsa = min(128, S)
    tm = min(256, S)

    w_in_b = _prep_w_in(w_in)
    w_sb_b, w_nsa_b, w_diff_b = w_sb_out.astype(BF16), w_nsa_out.astype(BF16), w_diff_out.astype(BF16)
    w_o_b, w_up_b, w_down_b = w_o.astype(BF16), w_ff_up.astype(BF16), w_ff_down.astype(BF16)
    w_sb_s = _padded_out_weight(w_sb_b, SB_HEADS, 4 * HEAD_DIM, HEAD_DIM, lambda h: (h // 2) * HEAD_DIM)
    w_nsa_s = _padded_out_weight(w_nsa_b, NSA_HEADS, LANES, HEAD_DIM, lambda h: (h // NSA_REP) * HEAD_DIM)
    w_diff_s = _padded_out_weight(w_diff_b, 2 * DIFF_HEADS, 2 * DIFF_VDIM, DIFF_VDIM, lambda h: (h // 2) * DIFF_VDIM)

    rope_p = _rope_tables(jnp.arange(S, dtype=jnp.int32))
    rope_s = _rope_tables(jnp.full((NB_S,), past, dtype=jnp.int32))
    assert page == LANES
    n_pool = cache_sb_kv.shape[1]
    feature_major = lambda c: jnp.transpose(c, (0, 1, 3, 4, 5, 2))
    cache_sb_t = feature_major(cache_sb_kv).reshape(depth, n_pool, -1, page)
    cache_nsa_t = feature_major(cache_nsa_kv).reshape(depth, n_pool, -1, page)
    cache_diff_r = cache_diff_kv.reshape(depth, n_pool, -1, DIFF_VDIM)
    wlen = state_nsa_win.shape[2]
    win_state_t = feature_major(state_nsa_win).reshape(depth, NB_S, -1, wlen)

    xp = x_prompt.reshape(T, D)
    xs = x_sample.reshape(NB_S, D)
    yp = ys = None
    outs = {k: [] for k in ("sb_p", "sb_s", "nsa_p", "nsa_s", "diff_p", "diff_s", "win_p", "win_s", "conv_p", "conv_s")}
    for l in range(depth):
        lam_init = 0.8 - 0.6 * math.exp(-0.3 * l)
        consts = jnp.full((HALO, LANES), lam_init, F32)
        g_attn = ln_attn[l].reshape(1, D)
        g_ffn = ln_ffn[l].reshape(1, D)
        g_fin = ln_final.reshape(1, D)
        subln = diff_subln[l].reshape(1, DIFF_VDIM)
        pe2, wk2, wv2 = _compress_weights(nsa_cmp_pe[l], nsa_cmp_wk[l], nsa_cmp_wv[l])
        conv_b = ffn_conv_b[l].reshape(1, dff)

        (sbq, sbkv, sbkv_b, nsaq, nsakv, nsakv_b, winkv, winkv_b, nsag, dq, dkv, dkv_b, mg) = _inproj(
            xp, g_attn, w_in_b[l], rope_p, tm, S // tm)
        o_sb = _sb_prompt(sbq, sbkv_b, B, S, tq)
        o_diff = _diff_prompt(dq, dkv_b, diff_lambda[l], subln, consts, B, S, tq)
        cb = _compress_prompt(nsakv, pe2, wk2, wv2, B, S)
        o_nsa = _nsa_prompt(nsaq, nsag, cb, nsakv_b, winkv_b, B, S, tq_nsa)
        xp = _merge(xp, mg, o_sb, o_nsa, o_diff, w_sb_b[l], w_nsa_b[l], w_diff_b[l], w_o_b[l], tm)
        xp, tails, yp = _ffn_prompt(xp, g_ffn, w_up_b[l], ffn_conv_w[l], conv_b, w_down_b[l], g_fin, S, tm)
        outs["sb_p"].append(sbkv.reshape(B, S, 2, SB_KV, HEAD_DIM))
        outs["nsa_p"].append(nsakv.reshape(B, S, 4, NSA_KV, HEAD_DIM))
        outs["diff_p"].append(dkv.reshape(B, S, 2, DIFF_KV, DIFF_VDIM))
        keep = min(NSA_WINDOW, S)
        outs["win_p"].append(winkv.reshape(B, S, 2, NSA_KV, HEAD_DIM)[:, S - keep:])
        outs["conv_p"].append(tails.reshape(B, S // tm, HALO, dff)[:, -1, HALO - (CONV_W - 1):])

        (sbq, sbkv, _, nsaq, nsakv, _, winkv, _, nsag, dq, dkv, _, mg) = _inproj(
            xs, g_attn, w_in_b[l], rope_s, NB_S, 1)
        o_sb = _sb_decode(cache_sb_t, l, page_table, sbq.reshape(NB_S, SB_HEADS, LANES))
        dq4 = dq.reshape(NB_S, DIFF_HEADS, LANES).astype(F32)
        lane = jnp.arange(LANES) < HEAD_DIM
        dq8 = jnp.concatenate([jnp.where(lane, dq4, 0.0), jnp.where(lane, 0.0, dq4)], axis=1).astype(BF16)
        o_diff = _diff_decode(cache_diff_r, l, page_table, dq8, dkv.reshape(NB_S, 1, -1),
                              diff_lambda[l], subln, consts)
        cb = _compress_decode(cache_nsa_t, l, page_table, pe2, wk2, wv2, min(8, NB_S))
        o_nsa = _nsa_decode(cache_nsa_t, l, page_table, nsaq.reshape(NB_S, NSA_HEADS, LANES),
                            nsag.reshape(NB_S, 1, LANES), cb.reshape(NB_S, past // NSA_BLOCK, 2 * LANES),
                            nsakv.reshape(NB_S, 1, -1), win_state_t, winkv.reshape(NB_S, 1, -1))
        xs = _merge(xs, mg, o_sb.reshape(NB_S, -1), o_nsa.reshape(NB_S, -1), o_diff.reshape(NB_S, -1),
                    w_sb_s[l], w_nsa_s[l], w_diff_s[l], w_o_b[l], NB_S)
        prev = state_ffn_conv[l]
        xs, a_new, ys = _ffn_sample(xs, prev[:, 0], prev[:, 1], g_ffn, w_up_b[l], ffn_conv_w[l], conv_b,
                                    w_down_b[l], g_fin)
        outs["sb_s"].append(sbkv.reshape(NB_S, 1, 2, SB_KV, HEAD_DIM))
        outs["nsa_s"].append(nsakv.reshape(NB_S, 1, 4, NSA_KV, HEAD_DIM))
        outs["diff_s"].append(dkv.reshape(NB_S, 1, 2, DIFF_KV, DIFF_VDIM))
        win_full = jnp.concatenate([state_nsa_win[l], winkv.reshape(NB_S, 1, 2, NSA_KV, HEAD_DIM)], axis=1)
        outs["win_s"].append(win_full[:, win_full.shape[1] - wlen:])
        outs["conv_s"].append(jnp.concatenate([prev[:, 1:], a_new[:, None]], axis=1))

    st = lambda k: jnp.stack(outs[k])
    return (yp.reshape(B, S, D), ys.reshape(NB_S, 1, D),
            st("sb_p"), st("sb_s"), st("nsa_p"), st("nsa_s"), st("diff_p"), st("diff_s"),
            st("win_p"), st("win_s"), st("conv_p"), st("conv_s"))
```

```python
import functools
import math

import jax
import jax.numpy as jnp
from jax import lax
from jax.experimental import pallas as pl
from jax.experimental.pallas import tpu as pltpu

F32 = jnp.float32
BF16 = jnp.bfloat16

HEAD_DIM = 64
SB_HEADS, SB_KV = 8, 4
NSA_HEADS, NSA_KV = 8, 2
NSA_REP = NSA_HEADS // NSA_KV
NSA_BLOCK = 64
NSA_SEL = 16
NSA_WINDOW = 512
DIFF_HEADS, DIFF_KV = 4, 2
DIFF_VDIM = 2 * HEAD_DIM
ROPE_THETA = 500000.0
ROT_DIM = HEAD_DIM // 4
CONV_W = 3
EPS = 1e-6
NEG_INF = -1e30
F32_EXP_UNDERFLOW = -104.0
SCALE = HEAD_DIM ** -0.5

LANES = 128
HALO = 8

C_SBQ = (0, 512)
C_SBKV = (512, 1024)
C_NSAQ = (1024, 1536)
C_NSAKV = (1536, 2048)
C_WIN = (2048, 2304)
C_NSAG = (2304, 2432)
C_DQ = (2432, 2944)
C_DKV = (2944, 3456)
C_MG = 3456
N_GATE_RAW = 3 * NSA_HEADS

VMEM_LIMIT = 56 * 1024 * 1024


def _cparams(sem):
    return pltpu.CompilerParams(dimension_semantics=sem, vmem_limit_bytes=VMEM_LIMIT)


def _dot(a, b):
    return jnp.dot(a, b, preferred_element_type=F32)


def _dot_nt(a, b):
    return lax.dot_general(a, b, (((1,), (1,)), ((), ())), preferred_element_type=F32)


def _rms(x, g):
    ms = jnp.mean(x * x, axis=-1, keepdims=True)
    return x * lax.rsqrt(ms + EPS) * g


def _sigmoid(x):
    return 1.0 / (1.0 + jnp.exp(-x))


def _log_keep(z):
    return -(jnp.maximum(z, 0.0) + jnp.log(1.0 + jnp.exp(-jnp.abs(z))))


def _resident(shape, index_map):
    return pl.BlockSpec(shape, index_map, pipeline_mode=pl.Buffered(1))


def _inproj_kernel(x_ref, g_ref, w_ref, cos_ref, s1_ref, s2_ref,
                   sbq_ref, sbkv_ref, sbkvb_ref, nsaq_ref, nsakv_ref, nsakvb_ref,
                   winkv_ref, winkvb_ref, nsag_ref, dq_ref, dkv_ref, dkvb_ref, mg_ref):
    tm = x_ref.shape[0]
    hb = _rms(x_ref[...], g_ref[...]).astype(BF16)
    cos, s1, s2 = cos_ref[...], s1_ref[...], s2_ref[...]
    lo_half = lax.broadcasted_iota(jnp.int32, (tm, LANES), 1) < HEAD_DIM

    def mm(c):
        return _dot(hb, w_ref[:, c[0]:c[1]])

    def rope(u):
        return u * cos + pltpu.roll(u, LANES - ROT_DIM // 2, 1) * s1 + pltpu.roll(u, ROT_DIM // 2, 1) * s2

    def tiles(u):
        return [u[:, t * LANES:(t + 1) * LANES] for t in range(u.shape[1] // LANES)]

    def padded_heads(u, half_of_head, do_rope):
        out = []
        for t, ut in enumerate(tiles(u)):
            if do_rope:
                ut = rope(ut)
            ut = ut * SCALE
            ur = pltpu.roll(ut, HEAD_DIM, 1)
            for hh in range(2):
                tgt = half_of_head(2 * t + hh)
                src = ut if tgt == hh else ur
                out.append(jnp.where(lo_half if tgt == 0 else jnp.logical_not(lo_half), src, 0.0))
        return jnp.concatenate(out, axis=1).astype(BF16)

    sbq_ref[...] = padded_heads(mm(C_SBQ), lambda h: (h // 2) % 2, False)
    u = mm(C_SBKV)
    sbkv_ref[...] = u
    sbkvb_ref[...] = u.astype(BF16)
    nsaq_ref[...] = padded_heads(mm(C_NSAQ), lambda h: h // NSA_REP, True)
    t = tiles(mm(C_NSAKV))
    u = jnp.concatenate([rope(t[0]), t[1], rope(t[2]), t[3]], axis=1)
    nsakv_ref[...] = u
    nsakvb_ref[...] = u.astype(BF16)
    t = tiles(mm(C_WIN))
    u = jnp.concatenate([rope(t[0]), t[1]], axis=1)
    winkv_ref[...] = u
    winkvb_ref[...] = u.astype(BF16)
    nsag_ref[...] = _sigmoid(mm(C_NSAG))
    dq_ref[...] = (jnp.concatenate([rope(ut) for ut in tiles(mm(C_DQ))], axis=1) * SCALE).astype(BF16)
    t = tiles(mm(C_DKV))
    u = jnp.concatenate([rope(t[0]), rope(t[1]), t[2], t[3]], axis=1)
    dkv_ref[...] = u
    dkvb_ref[...] = u.astype(BF16)
    d3 = mg_ref.shape[1]
    for c in range(3):
        mg_ref[:, c * (d3 // 3):(c + 1) * (d3 // 3)] = _sigmoid(
            mm((C_MG + c * (d3 // 3), C_MG + (c + 1) * (d3 // 3))))


def _inproj(x2d, g, w, rope_tabs, tm, n_pos_tiles):
    T, D = x2d.shape
    n_w = w.shape[1]
    row = lambda i: (i, 0)
    tab = lambda i: (i % n_pos_tiles, 0)
    widths = [(1024, BF16), (512, F32), (512, BF16), (1024, BF16), (512, F32), (512, BF16),
              (256, F32), (256, BF16), (LANES, F32), (512, BF16), (512, F32), (512, BF16), (3 * D, F32)]
    return pl.pallas_call(
        _inproj_kernel,
        grid=(T // tm,),
        in_specs=[pl.BlockSpec((tm, D), row), _resident((1, D), lambda i: (0, 0)),
                  _resident((D, n_w), lambda i: (0, 0)),
                  pl.BlockSpec((tm, LANES), tab), pl.BlockSpec((tm, LANES), tab), pl.BlockSpec((tm, LANES), tab)],
        out_specs=[pl.BlockSpec((tm, wd), row) for wd, _ in widths],
        out_shape=[jax.ShapeDtypeStruct((T, wd), dt) for wd, dt in widths],
        compiler_params=_cparams(("parallel",)),
        name="inproj",
    )(x2d, g, w, *rope_tabs)


def _rope_tables(pos):
    half = ROT_DIM // 2
    inv_freq = ROPE_THETA ** (-jnp.arange(half, dtype=F32) * 2.0 / ROT_DIM)
    ang = pos.astype(F32)[:, None] * inv_freq
    cos, sin = jnp.cos(ang), jnp.sin(ang)
    lane = jnp.arange(LANES) % HEAD_DIM
    idx = lane % half
    c = jnp.where(lane < ROT_DIM, cos[:, idx], 1.0)
    s1 = jnp.where(lane < half, -sin[:, idx], 0.0)
    s2 = jnp.where((lane >= half) & (lane < ROT_DIM), sin[:, idx], 0.0)
    return c, s1, s2


def _sb_kernel(q_ref, k_ref, v_ref, o_ref):
    tq = q_ref.shape[0]
    tk = tq
    i = pl.program_id(2)
    n_heads = q_ref.shape[1] // LANES
    rows = n_heads * tq
    q2 = jnp.concatenate([q_ref[:, h * LANES:(h + 1) * LANES] for h in range(n_heads)], axis=0)
    tri = (lax.broadcasted_iota(jnp.int32, (tk, tk), 0) > lax.broadcasted_iota(jnp.int32, (tk, tk), 1)).astype(BF16)
    tri2 = jnp.concatenate([tri, tri], axis=0)

    def block(j, carry, diagonal):
        acc, run = carry
        k = k_ref[pl.ds(pl.multiple_of(j * tk, tk), tk), :]
        v = v_ref[pl.ds(pl.multiple_of(j * tk, tk), tk), :]
        z = _dot_nt(q2, k)
        lk = _log_keep(z)
        if diagonal:
            qrow = lax.broadcasted_iota(jnp.int32, (rows, tk), 0) % tq
            mask = lax.broadcasted_iota(jnp.int32, (rows, tk), 1) < qrow
            lk = jnp.where(mask, lk, 0.0)
        hi = lk.astype(BF16)
        lo = (lk - hi.astype(F32)).astype(BF16)
        between = _dot(jnp.concatenate([hi, lo], axis=1), tri2)
        e = jnp.exp(lk + z + between + run)
        if diagonal:
            e = jnp.where(mask, e, 0.0)
        acc = acc + _dot(e.astype(BF16), v)
        run = run + jnp.sum(lk, axis=1, keepdims=True)
        return acc, run

    acc, run = block(i, (jnp.zeros((rows, LANES), F32), jnp.zeros((rows, 1), F32)), True)

    def more(state):
        t, _, run = state
        return jnp.logical_and(t < i, jnp.max(run) > F32_EXP_UNDERFLOW)

    def step(state):
        t, acc, run = state
        acc, run = block(i - 1 - t, (acc, run), False)
        return t + 1, acc, run

    _, acc, _ = lax.while_loop(more, step, (jnp.int32(0), acc, run))
    a = [acc[h * tq:(h + 1) * tq] for h in range(n_heads)]
    lo_half = lax.broadcasted_iota(jnp.int32, (tq, LANES), 1) < HEAD_DIM
    first = jnp.where(lo_half, a[0], pltpu.roll(a[1], HEAD_DIM, 1))
    second = jnp.where(lo_half, pltpu.roll(a[2], HEAD_DIM, 1), a[3])
    o_ref[...] = jnp.concatenate([first, second], axis=1).astype(o_ref.dtype)


def _sb_prompt(q_pad, kv_b, B, S, tq):
    T = B * S
    nq = S // tq
    n_pairs = SB_KV // 2
    return pl.pallas_call(
        _sb_kernel,
        grid=(B, n_pairs, nq),
        in_specs=[pl.BlockSpec((tq, 4 * LANES), lambda b, p, i: (b * nq + i, p)),
                  pl.BlockSpec((S, LANES), lambda b, p, i: (b, p)),
                  pl.BlockSpec((S, LANES), lambda b, p, i: (b, n_pairs + p))],
        out_specs=pl.BlockSpec((tq, 2 * LANES), lambda b, p, i: (b * nq + i, p)),
        out_shape=jax.ShapeDtypeStruct((T, SB_HEADS * HEAD_DIM), BF16),
        compiler_params=_cparams(("parallel", "parallel", "arbitrary")),
        name="sb_prompt",
    )(q_pad, kv_b, kv_b)


def _diff_lambda(lamv_ref, c_ref):
    lv = lamv_ref[...]
    a = jnp.sum(lv[0:1] * lv[1:2], axis=1, keepdims=True)
    b = jnp.sum(lv[2:3] * lv[3:4], axis=1, keepdims=True)
    lam_init = c_ref[0:1, 0:1]
    return jnp.exp(a) - jnp.exp(b) + lam_init, lam_init


def _softmax_step(s, mask, carry, v):
    m, l, acc = carry
    if mask is not None:
        s = jnp.where(mask, s, NEG_INF)
    m_new = jnp.maximum(m, jnp.max(s, axis=1, keepdims=True))
    p = jnp.exp(s - m_new)
    if mask is not None:
        p = jnp.where(mask, p, 0.0)
    alpha = jnp.exp(m - m_new)
    l = alpha * l + jnp.sum(p, axis=1, keepdims=True)
    acc = alpha * acc + _dot(p.astype(BF16), v)
    return m_new, l, acc


def _softmax_init(rows, width):
    return (jnp.full((rows, 1), NEG_INF, F32), jnp.zeros((rows, 1), F32), jnp.zeros((rows, width), F32))


def _diff_kernel(q_ref, k_ref, v_ref, lamv_ref, subln_ref, c_ref, o_ref):
    tq = q_ref.shape[0]
    tk = min(2 * tq, k_ref.shape[0])
    i = pl.program_id(2)
    j_diag = (i * tq) // tk
    rows = 4 * tq
    lo_half = lax.broadcasted_iota(jnp.int32, (tq, LANES), 1) < HEAD_DIM
    qs = []
    for m in range(2):
        for h in range(2):
            qh = q_ref[:, h * LANES:(h + 1) * LANES].astype(F32)
            qs.append(jnp.where(lo_half if m == 0 else jnp.logical_not(lo_half), qh, 0.0).astype(BF16))
    q4 = jnp.concatenate(qs, axis=0)

    def block(j, carry, diagonal):
        k = k_ref[pl.ds(pl.multiple_of(j * tk, tk), tk), :]
        v = v_ref[pl.ds(pl.multiple_of(j * tk, tk), tk), :]
        s = _dot_nt(q4, k)
        mask = None
        if diagonal:
            r = lax.broadcasted_iota(jnp.int32, (rows, tk), 0)
            mask = j * tk + lax.broadcasted_iota(jnp.int32, (rows, tk), 1) <= i * tq + (r % tq)
        return _softmax_step(s, mask, carry, v)

    carry = block(j_diag, _softmax_init(rows, LANES), True)
    _, l, acc = lax.fori_loop(0, j_diag, lambda t, c: block(t, c, False), carry)
    a = acc / l
    lam, lam_init = _diff_lambda(lamv_ref, c_ref)
    outs = []
    for h in range(2):
        o = a[h * tq:(h + 1) * tq] - lam * a[(2 + h) * tq:(3 + h) * tq]
        outs.append(_rms(o, subln_ref[...]) * (1.0 - lam_init))
    o_ref[...] = jnp.concatenate(outs, axis=1).astype(o_ref.dtype)


def _diff_prompt(q_b, kv_b, lamv, subln, consts, B, S, tq):
    T = B * S
    nq = S // tq
    whole = lambda b, g, i: (0, 0)
    return pl.pallas_call(
        _diff_kernel,
        grid=(B, DIFF_KV, nq),
        in_specs=[pl.BlockSpec((tq, 2 * LANES), lambda b, g, i: (b * nq + i, g)),
                  pl.BlockSpec((S, LANES), lambda b, g, i: (b, g)),
                  pl.BlockSpec((S, LANES), lambda b, g, i: (b, 2 + g)),
                  pl.BlockSpec(lamv.shape, whole), pl.BlockSpec(subln.shape, whole),
                  pl.BlockSpec(consts.shape, whole)],
        out_specs=pl.BlockSpec((tq, 2 * LANES), lambda b, g, i: (b * nq + i, g)),
        out_shape=jax.ShapeDtypeStruct((T, DIFF_HEADS * DIFF_VDIM), BF16),
        compiler_params=_cparams(("parallel", "parallel", "arbitrary")),
        name="diff_prompt",
    )(q_b, kv_b, kv_b, lamv, subln, consts)


def _compress_blocks(buf_k, buf_v, pe_ref, wk_ref, wv_ref, o_ref):
    nblk = buf_k.shape[0] // NSA_BLOCK

    def body(j, acc):
        pe = pe_ref[pl.ds(j, 1), :]
        xk = (buf_k[pl.ds(j, nblk, stride=NSA_BLOCK), :] + pe).astype(BF16)
        xv = (buf_v[pl.ds(j, nblk, stride=NSA_BLOCK), :] + pe).astype(BF16)
        return acc + jnp.concatenate([_dot(xk, wk_ref[j]), _dot(xv, wv_ref[j])], axis=1)

    o_ref[...] = lax.fori_loop(0, NSA_BLOCK, body, jnp.zeros((nblk, 2 * LANES), F32)).astype(o_ref.dtype)


def _compress_prompt_kernel(x_ref, pe_ref, wk_ref, wv_ref, o_ref, buf_k, buf_v):
    buf_k[...] = x_ref[:, :LANES]
    buf_v[...] = x_ref[:, LANES:]
    _compress_blocks(buf_k, buf_v, pe_ref, wk_ref, wv_ref, o_ref)


def _compress_decode_kernel(pt_ref, pe_ref, wk_ref, wv_ref, *refs, seqs_per_block):
    *pages, o_ref, buf_k, buf_v = refs
    page = pages[0].shape[1]
    t = pl.program_id(0) % seqs_per_block
    for k, p_ref in enumerate(pages):
        x = p_ref[...].T
        dst = pl.ds(pl.multiple_of((t * len(pages) + k) * page, page), page)
        buf_k[dst, :] = x[:, :LANES]
        buf_v[dst, :] = x[:, LANES:]

    @pl.when(t == seqs_per_block - 1)
    def _():
        _compress_blocks(buf_k, buf_v, pe_ref, wk_ref, wv_ref, o_ref)


def _compress_weights(pe, wk, wv):
    eye = jnp.eye(NSA_KV, dtype=F32)
    bd = lambda w: jnp.einsum('ab,jde->jadbe', eye, w).reshape(NSA_BLOCK, LANES, LANES).astype(BF16)
    return jnp.tile(pe, (1, NSA_KV)), bd(wk), bd(wv)


def _compress_prompt(nsakv, pe2, wk2, wv2, B, S):
    chunk = min(S, 2048)
    nc = S // chunk
    whole2 = lambda b: (0, 0)
    whole3 = lambda b: (0, 0, 0)
    return pl.pallas_call(
        _compress_prompt_kernel,
        grid=(B * nc,),
        in_specs=[pl.BlockSpec((chunk, 2 * LANES), lambda b: (b, 0)),
                  pl.BlockSpec(pe2.shape, whole2), pl.BlockSpec(wk2.shape, whole3), pl.BlockSpec(wv2.shape, whole3)],
        out_specs=pl.BlockSpec((chunk // NSA_BLOCK, 2 * LANES), lambda b: (b, 0)),
        out_shape=jax.ShapeDtypeStruct((B * S // NSA_BLOCK, 2 * LANES), BF16),
        scratch_shapes=[pltpu.VMEM((chunk, LANES), F32)] * 2,
        compiler_params=_cparams(("parallel",)),
        name="compress_prompt",
    )(nsakv, pe2, wk2, wv2)


def _page_specs(npg, layer, rows, row_block):
    def spec(k):
        return pl.BlockSpec((None, None, rows, LANES), lambda b, pt: (layer, pt[b, k], row_block, 0))
    return [spec(k) for k in range(npg)]


def _compress_decode(cache_t, layer, page_table, pe2, wk2, wv2, seqs_per_block):
    nb, npg = page_table.shape
    page = cache_t.shape[3]
    sg = seqs_per_block
    rows = sg * npg * page
    whole2 = lambda b, pt: (0, 0)
    whole3 = lambda b, pt: (0, 0, 0)
    grid_spec = pltpu.PrefetchScalarGridSpec(
        num_scalar_prefetch=1,
        grid=(nb,),
        in_specs=[pl.BlockSpec(pe2.shape, whole2), pl.BlockSpec(wk2.shape, whole3), pl.BlockSpec(wv2.shape, whole3)]
        + _page_specs(npg, layer, 2 * LANES, 0),
        out_specs=pl.BlockSpec((rows // NSA_BLOCK, 2 * LANES), lambda b, pt: (b // sg, 0)),
        scratch_shapes=[pltpu.VMEM((rows, LANES), F32)] * 2,
    )
    return pl.pallas_call(
        functools.partial(_compress_decode_kernel, seqs_per_block=sg),
        grid_spec=grid_spec,
        out_shape=jax.ShapeDtypeStruct((nb * npg * page // NSA_BLOCK, 2 * LANES), BF16),
        compiler_params=_cparams(("arbitrary",)),
        name="compress_decode",
    )(page_table, pe2, wk2, wv2, *([cache_t] * npg))


def _top_blocks(imp, n_top, n_cand):
    col = lax.broadcasted_iota(jnp.int32, imp.shape, 1)

    def body(t, carry):
        vals, sel = carry
        hit = col == jnp.argmax(vals, axis=1, keepdims=True)
        sel = jnp.where(jnp.logical_and(hit, t < n_cand), 1.0, sel)
        vals = jnp.where(hit, -2.0, vals)
        return vals, sel

    _, sel = lax.fori_loop(0, n_top, body, (imp, jnp.zeros(imp.shape, F32)))
    return sel


def _compressed_branch(q, kc, vc, n_valid_of_row, cur_of_row, group_rows):
    sc = _dot_nt(q, kc)
    col = lax.broadcasted_iota(jnp.int32, sc.shape, 1)
    cmask = col < n_valid_of_row
    mx = jnp.max(jnp.where(cmask, sc, NEG_INF), axis=1, keepdims=True)
    e = jnp.where(cmask, jnp.exp(sc - mx), 0.0)
    l = jnp.sum(e, axis=1, keepdims=True)
    pc = e / jnp.where(l > 0.0, l, 1.0)
    return _dot(pc.astype(BF16), vc), pc


def _gate(gates, head, branch, rows):
    c = head * 3 + branch
    return gates[:, c:c + 1]


def _nsa_kernel(q_ref, gate_ref, cb_ref, sel_ref, win_ref, onehot_ref, o_ref, *, n_top):
    tq = q_ref.shape[0]
    nb = cb_ref.shape[0]
    tk = min(4 * LANES, sel_ref.shape[0])
    tkw = min(2 * LANES, sel_ref.shape[0])
    i = pl.program_id(1)
    q0 = i * tq
    j_diag = q0 // tk
    rows = NSA_HEADS * tq
    pos1 = q0 + lax.broadcasted_iota(jnp.int32, (tq, 1), 0)
    pos8 = q0 + lax.broadcasted_iota(jnp.int32, (rows, 1), 0) % tq
    kc, vc = cb_ref[:, :LANES], cb_ref[:, LANES:]
    q8 = jnp.concatenate([q_ref[:, h * LANES:(h + 1) * LANES] for h in range(NSA_HEADS)], axis=0)
    o_cmp, pc = _compressed_branch(q8, kc, vc, (pos8 + 1) // NSA_BLOCK, None, None)
    imps = []
    for g in range(NSA_KV):
        imp = pc[NSA_REP * g * tq:(NSA_REP * g + 1) * tq]
        for r in range(1, NSA_REP):
            imp = imp + pc[(NSA_REP * g + r) * tq:(NSA_REP * g + r + 1) * tq]
        imps.append(imp)
    cur2 = jnp.concatenate([pos1, pos1], axis=0) // NSA_BLOCK
    col2 = lax.broadcasted_iota(jnp.int32, (2 * tq, nb), 1)
    imp2 = jnp.where(col2 < cur2, jnp.concatenate(imps, axis=0), -1.0)
    sel2 = jnp.where(col2 == cur2, 1.0, _top_blocks(imp2, n_top, cur2))

    gates = gate_ref[...]
    lo_half = lax.broadcasted_iota(jnp.int32, (tq, LANES), 1) < HEAD_DIM
    n_win_blocks = (q0 + tq - 1) // tkw + 1
    w_lo = jnp.maximum(q0 - NSA_WINDOW, 0) // tkw
    m8 = jnp.concatenate([sel2[g * tq:(g + 1) * tq] for g in range(NSA_KV) for _ in range(NSA_REP)], axis=0)
    qx = jnp.concatenate([q8, ((m8 - 1.0) * -NEG_INF).astype(BF16)], axis=1)

    def sel_scores(j):
        k0 = pl.multiple_of(j * tk, tk)
        kx = jnp.concatenate([sel_ref[pl.ds(k0, tk), :LANES], onehot_ref[pl.ds(k0, tk), :]], axis=1)
        return _dot_nt(qx, kx), sel_ref[pl.ds(k0, tk), LANES:]

    s, v = sel_scores(j_diag)
    kpos = j_diag * tk + lax.broadcasted_iota(jnp.int32, (rows, tk), 1)
    carry = _softmax_step(jnp.where(kpos <= pos8, s, NEG_INF), None, _softmax_init(rows, LANES), v)

    def sel_block(j, carry):
        s, v = sel_scores(j)
        return _softmax_step(s, None, carry, v)

    _, l, acc = lax.fori_loop(0, j_diag, sel_block, carry)
    o_sel = acc / l

    def win_block(j, carry):
        k0 = pl.multiple_of(j * tkw, tkw)
        k = win_ref[pl.ds(k0, tkw), :LANES]
        v = win_ref[pl.ds(k0, tkw), LANES:]
        s = _dot_nt(q8, k)
        kpos = k0 + lax.broadcasted_iota(jnp.int32, (rows, tkw), 1)
        return _softmax_step(s, jnp.logical_and(kpos <= pos8, kpos >= pos8 - NSA_WINDOW), carry, v)

    _, l, acc = lax.fori_loop(w_lo, n_win_blocks, win_block, _softmax_init(rows, LANES))
    o_win = acc / l

    heads = []
    for h in range(NSA_HEADS):
        sl = slice(h * tq, (h + 1) * tq)
        heads.append(_gate(gates, h, 0, tq) * o_cmp[sl] + _gate(gates, h, 1, tq) * o_sel[sl]
                     + _gate(gates, h, 2, tq) * o_win[sl])
    out_tiles = []
    for t in range(NSA_HEADS // 2):
        a, b = heads[2 * t], heads[2 * t + 1]
        if (2 * t) // NSA_REP == 0:
            out_tiles.append(jnp.where(lo_half, a, pltpu.roll(b, HEAD_DIM, 1)))
        else:
            out_tiles.append(jnp.where(lo_half, pltpu.roll(a, HEAD_DIM, 1), b))
    o_ref[...] = jnp.concatenate(out_tiles, axis=1).astype(o_ref.dtype)


def _nsa_prompt(q_pad, gates, cb, nsakv_b, winkv_b, B, S, tq):
    T = B * S
    nq = S // tq
    nb = S // NSA_BLOCK
    onehot = (jnp.arange(S)[:, None] // NSA_BLOCK == jnp.arange(nb)[None, :]).astype(BF16)
    return pl.pallas_call(
        functools.partial(_nsa_kernel, n_top=min(NSA_SEL - 1, nb)),
        grid=(B, nq),
        in_specs=[pl.BlockSpec((tq, NSA_HEADS * LANES), lambda b, i: (b * nq + i, 0)),
                  pl.BlockSpec((tq, LANES), lambda b, i: (b * nq + i, 0)),
                  pl.BlockSpec((nb, 2 * LANES), lambda b, i: (b, 0)),
                  pl.BlockSpec((S, 2 * LANES), lambda b, i: (b, 1)),
                  pl.BlockSpec((S, 2 * LANES), lambda b, i: (b, 0)),
                  _resident((S, nb), lambda b, i: (0, 0))],
        out_specs=pl.BlockSpec((tq, NSA_HEADS * HEAD_DIM), lambda b, i: (b * nq + i, 0)),
        out_shape=jax.ShapeDtypeStruct((T, NSA_HEADS * HEAD_DIM), BF16),
        compiler_params=_cparams(("parallel", "arbitrary")),
        name="nsa_prompt",
    )(q_pad, gates, cb, nsakv_b, winkv_b, onehot)


def _merge_kernel(x_ref, mg_ref, osb_ref, onsa_ref, odiff_ref, wsb_ref, wnsa_ref, wdiff_ref, wo_ref, y_ref):
    d = x_ref.shape[1]
    merged = (mg_ref[:, 0:d] * _dot(osb_ref[...].astype(BF16), wsb_ref[...])
              + mg_ref[:, d:2 * d] * _dot(onsa_ref[...].astype(BF16), wnsa_ref[...])
              + mg_ref[:, 2 * d:3 * d] * _dot(odiff_ref[...].astype(BF16), wdiff_ref[...]))
    y_ref[...] = x_ref[...] + _dot(merged.astype(BF16), wo_ref[...])


def _merge(x2d, mg, o_sb, o_nsa, o_diff, w_sb, w_nsa, w_diff, w_o, tm):
    T, D = x2d.shape
    row = lambda i: (i, 0)
    whole = lambda i: (0, 0)
    return pl.pallas_call(
        _merge_kernel,
        grid=(T // tm,),
        in_specs=[pl.BlockSpec((tm, D), row), pl.BlockSpec((tm, 3 * D), row),
                  pl.BlockSpec((tm, o_sb.shape[1]), row), pl.BlockSpec((tm, o_nsa.shape[1]), row),
                  pl.BlockSpec((tm, o_diff.shape[1]), row),
                  _resident(w_sb.shape, whole), _resident(w_nsa.shape, whole),
                  _resident(w_diff.shape, whole), _resident(w_o.shape, whole)],
        out_specs=pl.BlockSpec((tm, D), row),
        out_shape=jax.ShapeDtypeStruct((T, D), F32),
        compiler_params=_cparams(("parallel",)),
        name="merge",
    )(x2d, mg, o_sb, o_nsa, o_diff, w_sb, w_nsa, w_diff, w_o)


def _gelu(c):
    return 0.5 * c * (1.0 + jnp.tanh(math.sqrt(2.0 / math.pi) * (c + 0.044715 * (c * c * c))))


def _ffn_prompt_kernel(x_ref, halo_ref, g_ref, wup_ref, cw_ref, cb_ref, wdown_ref, gf_ref,
                       y_ref, tail_ref, yn_ref, a_scr, *, tiles_per_seq):
    tm = x_ref.shape[0]
    dff = cw_ref.shape[1]
    i = pl.program_id(0)
    x = x_ref[...]
    hb = _rms(x, g_ref[...]).astype(BF16)
    a = _dot(hb, wup_ref[:, :dff])
    gate = _dot(hb, wup_ref[:, dff:])
    a_halo = _dot(_rms(halo_ref[...], g_ref[...]).astype(BF16), wup_ref[:, :dff])
    a_scr[0:HALO, :] = jnp.where(i % tiles_per_seq == 0, 0.0, a_halo)
    a_scr[HALO:HALO + tm, :] = a
    c = cb_ref[...] + a * cw_ref[CONV_W - 1:CONV_W, :]
    for j in range(CONV_W - 1):
        off = HALO - (CONV_W - 1) + j
        c = c + a_scr[off:off + tm, :] * cw_ref[j:j + 1, :]
    y = x + _dot((_gelu(c) * gate).astype(BF16), wdown_ref[...])
    y_ref[...] = y
    yn_ref[...] = _rms(y, gf_ref[...])
    tail_ref[...] = a_scr[tm:tm + HALO, :]


def _ffn_prompt(x2d, g, w_up, conv_w, conv_b, w_down, g_final, S, tm):
    T, D = x2d.shape
    dff = conv_w.shape[1]
    whole = lambda i: (0, 0)
    hb = tm // HALO
    return pl.pallas_call(
        functools.partial(_ffn_prompt_kernel, tiles_per_seq=S // tm),
        grid=(T // tm,),
        in_specs=[pl.BlockSpec((tm, D), lambda i: (i, 0)),
                  pl.BlockSpec((HALO, D), lambda i: (jnp.maximum(i * hb - 1, 0), 0)),
                  _resident((1, D), whole), _resident(w_up.shape, whole), _resident(conv_w.shape, whole),
                  _resident(conv_b.shape, whole), _resident(w_down.shape, whole), _resident((1, D), whole)],
        out_specs=[pl.BlockSpec((tm, D), lambda i: (i, 0)),
                   pl.BlockSpec((None, HALO, dff), lambda i: (i, 0, 0)),
                   pl.BlockSpec((tm, D), lambda i: (i, 0))],
        out_shape=[jax.ShapeDtypeStruct((T, D), F32), jax.ShapeDtypeStruct((T // tm, HALO, dff), F32),
                   jax.ShapeDtypeStruct((T, D), F32)],
        scratch_shapes=[pltpu.VMEM((tm + HALO, dff), F32)],
        compiler_params=_cparams(("parallel",)),
        name="ffn_prompt",
    )(x2d, x2d, g, w_up, conv_w, conv_b, w_down, g_final)


def _ffn_sample_kernel(x_ref, p0_ref, p1_ref, g_ref, wup_ref, cw_ref, cb_ref, wdown_ref, gf_ref,
                       y_ref, a_ref, yn_ref):
    dff = cw_ref.shape[1]
    x = x_ref[...]
    hb = _rms(x, g_ref[...]).astype(BF16)
    a = _dot(hb, wup_ref[:, :dff])
    gate = _dot(hb, wup_ref[:, dff:])
    c = cb_ref[...] + p0_ref[...] * cw_ref[0:1, :] + p1_ref[...] * cw_ref[1:2, :] + a * cw_ref[2:3, :]
    y = x + _dot((_gelu(c) * gate).astype(BF16), wdown_ref[...])
    y_ref[...] = y
    a_ref[...] = a
    yn_ref[...] = _rms(y, gf_ref[...])


def _ffn_sample(x2d, prev0, prev1, g, w_up, conv_w, conv_b, w_down, g_final):
    T, D = x2d.shape
    dff = conv_w.shape[1]
    return pl.pallas_call(
        _ffn_sample_kernel,
        out_shape=[jax.ShapeDtypeStruct((T, D), F32), jax.ShapeDtypeStruct((T, dff), F32),
                   jax.ShapeDtypeStruct((T, D), F32)],
        compiler_params=pltpu.CompilerParams(vmem_limit_bytes=VMEM_LIMIT),
        name="ffn_sample",
    )(x2d, prev0, prev1, g, w_up, conv_w, conv_b, w_down, g_final)


def _head_rows(q_ref, split):
    q8 = q_ref[...].astype(F32)
    row = lax.broadcasted_iota(jnp.int32, q8.shape, 0)
    return jnp.concatenate([jnp.where(row < split, q8, 0.0), jnp.where(row >= split, q8, 0.0)], axis=1)


def _lane_cat(pages, r0, r1):
    return jnp.concatenate([p[r0:r1, :] for p in pages], axis=1).astype(BF16)


def _sb_decode_kernel(pt_ref, q_ref, *refs):
    *pages, o_ref = refs
    npg = len(pages)
    half = pages[0].shape[0] // 2
    tk = pages[0].shape[1]
    nh = q_ref.shape[0]
    q = _head_rows(q_ref, SB_HEADS // 2).astype(BF16)
    z = _dot(q, _lane_cat(pages, 0, half))
    zs = jnp.concatenate([z[:, k * tk:(k + 1) * tk] for k in range(npg)], axis=0)
    lk = _log_keep(zs)
    tri = (lax.broadcasted_iota(jnp.int32, (tk, tk), 0) > lax.broadcasted_iota(jnp.int32, (tk, tk), 1)).astype(BF16)
    hi = lk.astype(BF16)
    lo = (lk - hi.astype(F32)).astype(BF16)
    between = _dot(hi, tri) + _dot(lo, tri)
    tot = jnp.sum(lk, axis=1, keepdims=True)
    runs, run = [], jnp.zeros((nh, 1), F32)
    for k in reversed(range(npg)):
        runs.append(run)
        run = run + tot[k * nh:(k + 1) * nh]
    e = jnp.exp(lk + zs + between + jnp.concatenate(runs[::-1], axis=0))
    e = jnp.concatenate([e[k * nh:(k + 1) * nh] for k in range(npg)], axis=1).astype(BF16)
    acc = _dot_nt(e, _lane_cat(pages, half, 2 * half))
    row = lax.broadcasted_iota(jnp.int32, acc.shape, 0)
    lane = lax.broadcasted_iota(jnp.int32, acc.shape, 1)
    o_ref[...] = jnp.where(lane // HEAD_DIM == row // 2, acc, 0.0)


def _sb_decode(cache_t, layer, page_table, q_pad3):
    nb, npg = page_table.shape
    feat = cache_t.shape[2]
    grid_spec = pltpu.PrefetchScalarGridSpec(
        num_scalar_prefetch=1,
        grid=(nb,),
        in_specs=[pl.BlockSpec((None, SB_HEADS, LANES), lambda b, pt: (b, 0, 0))] + _page_specs(npg, layer, feat, 0),
        out_specs=pl.BlockSpec((None, SB_HEADS, feat // 2), lambda b, pt: (b, 0, 0)),
    )
    return pl.pallas_call(
        _sb_decode_kernel,
        grid_spec=grid_spec,
        out_shape=jax.ShapeDtypeStruct((nb, SB_HEADS, feat // 2), F32),
        compiler_params=_cparams(("parallel",)),
        name="sb_decode",
    )(page_table, q_pad3, *([cache_t] * npg))


def _softmax_with_new_key(s, s_new):
    m = jnp.maximum(jnp.max(s, axis=1, keepdims=True), s_new)
    p = jnp.exp(s - m)
    p_new = jnp.exp(s_new - m)
    return p.astype(BF16), p_new.astype(BF16).astype(F32), jnp.sum(p, axis=1, keepdims=True) + p_new


def _diff_decode_kernel(pt_ref, q_ref, new_ref, lamv_ref, subln_ref, c_ref, *refs):
    *pages, o_ref = refs
    page = pages[0].shape[0] // 4
    half = new_ref.shape[1] // 2
    row = lax.broadcasted_iota(jnp.int32, (2 * DIFF_HEADS, LANES), 0)
    q8 = q_ref[...].astype(F32)
    qf = jnp.concatenate([jnp.where((row % DIFF_HEADS) < 2, q8, 0.0), jnp.where((row % DIFF_HEADS) >= 2, q8, 0.0)], axis=1)

    def rows_of(slot):
        return jnp.concatenate(
            [jnp.concatenate([p[pl.ds(2 * slot + g, page, stride=4), :] for g in range(DIFF_KV)], axis=1)
             for p in pages], axis=0).astype(BF16)

    s = _dot_nt(qf.astype(BF16), rows_of(0))
    k_new = new_ref[:, :half].astype(BF16).astype(F32)
    v_new = new_ref[:, half:].astype(BF16).astype(F32)
    p, p_new, l = _softmax_with_new_key(s, jnp.sum(qf * k_new, axis=1, keepdims=True))
    a = (_dot(p, rows_of(1)) + p_new * v_new) / l
    lam, lam_init = _diff_lambda(lamv_ref, c_ref)
    o = a - lam * pltpu.roll(a, DIFF_HEADS, 0)
    r2 = lax.broadcasted_iota(jnp.int32, o.shape, 0)
    lane = lax.broadcasted_iota(jnp.int32, o.shape, 1)
    valid = jnp.logical_and(r2 < DIFF_HEADS, lane // DIFF_VDIM == r2 // 2)
    o = jnp.where(valid, o, 0.0)
    ms = jnp.sum(o * o, axis=1, keepdims=True) * (1.0 / DIFF_VDIM)
    g2 = jnp.concatenate([subln_ref[...], subln_ref[...]], axis=1)
    o_ref[...] = o * lax.rsqrt(ms + EPS) * g2 * (1.0 - lam_init)


def _diff_decode(cache_r, layer, page_table, q8, new_rows3, lamv, subln, consts):
    nb, npg = page_table.shape
    rows, width = cache_r.shape[2], new_rows3.shape[2]
    whole = lambda b, pt: (0, 0)
    grid_spec = pltpu.PrefetchScalarGridSpec(
        num_scalar_prefetch=1,
        grid=(nb,),
        in_specs=[pl.BlockSpec((None, 2 * DIFF_HEADS, LANES), lambda b, pt: (b, 0, 0)),
                  pl.BlockSpec((None, 1, width), lambda b, pt: (b, 0, 0)),
                  pl.BlockSpec(lamv.shape, whole), pl.BlockSpec(subln.shape, whole), pl.BlockSpec(consts.shape, whole)]
        + _page_specs(npg, layer, rows, 0),
        out_specs=pl.BlockSpec((None, 2 * DIFF_HEADS, width // 2), lambda b, pt: (b, 0, 0)),
    )
    return pl.pallas_call(
        _diff_decode_kernel,
        grid_spec=grid_spec,
        out_shape=jax.ShapeDtypeStruct((nb, 2 * DIFF_HEADS, width // 2), F32),
        compiler_params=_cparams(("parallel",)),
        name="diff_decode",
    )(page_table, q8, new_rows3, lamv, subln, consts, *([cache_r] * npg))


def _nsa_decode_kernel(pt_ref, q_ref, gate_ref, cb_ref, new_ref, win_ref, wnew_ref, *refs, n_top, n_cmp):
    *pages, o_ref = refs
    tk = pages[0].shape[1]
    past = len(pages) * tk
    rows = NSA_HEADS
    q = q_ref[...]
    qf = q.astype(F32)
    row1 = lax.broadcasted_iota(jnp.int32, (rows, 1), 0)
    bf = lambda x: x.astype(BF16).astype(F32)

    pad = jnp.zeros((LANES - n_cmp, 2 * LANES), BF16)
    cb = jnp.concatenate([cb_ref[...], pad], axis=0)
    o_cmp, pc = _compressed_branch(q, cb[:, :LANES], cb[:, LANES:], n_cmp, None, None)
    rowf = lax.broadcasted_iota(jnp.int32, pc.shape, 0)
    imp0 = jnp.sum(jnp.where(rowf < NSA_REP, pc, 0.0), axis=0, keepdims=True)
    imp1 = jnp.sum(jnp.where(rowf >= NSA_REP, pc, 0.0), axis=0, keepdims=True)
    col = lax.broadcasted_iota(jnp.int32, pc.shape, 1)
    imp = jnp.where(col < n_cmp, jnp.where(rowf < NSA_REP, imp0, imp1), -1.0)
    sel = _top_blocks(imp, n_top, n_cmp)

    blk = lax.broadcasted_iota(jnp.int32, (LANES, past), 1) // NSA_BLOCK
    onehot = (lax.broadcasted_iota(jnp.int32, (LANES, past), 0) == blk).astype(BF16)
    qx = jnp.concatenate([q, ((sel - 1.0) * -NEG_INF).astype(BF16)], axis=1)
    s = _dot(qx, jnp.concatenate([_lane_cat(pages, 0, LANES), onehot], axis=0))
    s_new = jnp.sum(qf * bf(new_ref[:, 2 * LANES:3 * LANES]), axis=1, keepdims=True)
    p, p_new, l = _softmax_with_new_key(s, s_new)
    o_sel = (_dot_nt(p, _lane_cat(pages, LANES, 2 * LANES)) + p_new * bf(new_ref[:, 3 * LANES:])) / l

    s = _dot(q, win_ref[:LANES, :].astype(BF16))
    s_new = jnp.sum(qf * bf(wnew_ref[:, :LANES]), axis=1, keepdims=True)
    p, p_new, l = _softmax_with_new_key(s, s_new)
    o_win = (_dot_nt(p, win_ref[LANES:, :].astype(BF16)) + p_new * bf(wnew_ref[:, LANES:])) / l

    gates = jnp.broadcast_to(gate_ref[...], (rows, LANES))
    lane = lax.broadcasted_iota(jnp.int32, (rows, LANES), 1)
    gsel = lambda c: jnp.sum(jnp.where(lane == 3 * row1 + c, gates, 0.0), axis=1, keepdims=True)
    o = gsel(0) * o_cmp + gsel(1) * o_sel + gsel(2) * o_win
    o_ref[...] = jnp.where(lane // HEAD_DIM == row1 // NSA_REP, o, 0.0)


def _nsa_decode(cache_t, layer, page_table, q8, gates3, cb3, new_rows3, win_state_t, wnew3):
    nb, npg = page_table.shape
    n_cmp = cb3.shape[1]
    wlen = win_state_t.shape[3]
    n_top = min(NSA_SEL - 1, n_cmp + 1)
    grid_spec = pltpu.PrefetchScalarGridSpec(
        num_scalar_prefetch=1,
        grid=(nb,),
        in_specs=[pl.BlockSpec((None, NSA_HEADS, LANES), lambda b, pt: (b, 0, 0)),
                  pl.BlockSpec((None, 1, LANES), lambda b, pt: (b, 0, 0)),
                  pl.BlockSpec((None, n_cmp, 2 * LANES), lambda b, pt: (b, 0, 0)),
                  pl.BlockSpec((None, 1, 4 * LANES), lambda b, pt: (b, 0, 0)),
                  pl.BlockSpec((None, None, 2 * LANES, wlen), lambda b, pt: (layer, b, 0, 0)),
                  pl.BlockSpec((None, 1, 2 * LANES), lambda b, pt: (b, 0, 0))]
        + _page_specs(npg, layer, 2 * LANES, 1),
        out_specs=pl.BlockSpec((None, NSA_HEADS, LANES), lambda b, pt: (b, 0, 0)),
    )
    return pl.pallas_call(
        functools.partial(_nsa_decode_kernel, n_top=n_top, n_cmp=n_cmp),
        grid_spec=grid_spec,
        out_shape=jax.ShapeDtypeStruct((nb, NSA_HEADS, LANES), F32),
        compiler_params=_cparams(("parallel",)),
        name="nsa_decode",
    )(page_table, q8, gates3, cb3, new_rows3, win_state_t, wnew3, *([cache_t] * npg))


def _prep_w_in(w_in):
    g0 = C_NSAG[0]
    gate = w_in[..., g0:g0 + N_GATE_RAW]
    gate = jnp.pad(gate, ((0, 0), (0, 0), (0, LANES - N_GATE_RAW)))
    return jnp.concatenate([w_in[..., :g0], gate, w_in[..., g0 + N_GATE_RAW:]], axis=-1).astype(BF16)


def _padded_out_weight(w, n_rows, row_width, head_width, lane_of_head):
    depth, _, d = w.shape
    out = jnp.zeros((depth, n_rows * row_width, d), w.dtype)
    for h in range(w.shape[1] // head_width):
        start = h * row_width + lane_of_head(h)
        out = out.at[:, start:start + head_width].set(w[:, h * head_width:(h + 1) * head_width])
    return out


def kernel(x_prompt, x_sample, cache_sb_kv, cache_nsa_kv, cache_diff_kv, state_nsa_win, state_ffn_conv, page_table, ln_attn, w_in, nsa_cmp_pe, nsa_cmp_wk, nsa_cmp_wv, diff_lambda, diff_subln, w_sb_out, w_nsa_out, w_diff_out, w_o, ln_ffn, w_ff_up, ffn_conv_w, ffn_conv_b, w_ff_down, ln_final):
    B, S, D = x_prompt.shape
    NB_S = x_sample.shape[0]
    depth = w_in.shape[0]
    npg = page_table.shape[1]
    page = cache_sb_kv.shape[2]
    past = npg * page
    dff = ffn_conv_w.shape[2]
    T = B * S
    tq = min(256, S)
    tq_nsa = min(128, S)
    tm = min(256, S)

    w_in_b = _prep_w_in(w_in)
    w_sb_b, w_nsa_b, w_diff_b = w_sb_out.astype(BF16), w_nsa_out.astype(BF16), w_diff_out.astype(BF16)
    w_o_b, w_up_b, w_down_b = w_o.astype(BF16), w_ff_up.astype(BF16), w_ff_down.astype(BF16)
    w_sb_s = _padded_out_weight(w_sb_b, SB_HEADS, 4 * HEAD_DIM, HEAD_DIM, lambda h: (h // 2) * HEAD_DIM)
    w_nsa_s = _padded_out_weight(w_nsa_b, NSA_HEADS, LANES, HEAD_DIM, lambda h: (h // NSA_REP) * HEAD_DIM)
    w_diff_s = _padded_out_weight(w_diff_b, 2 * DIFF_HEADS, 2 * DIFF_VDIM, DIFF_VDIM, lambda h: (h // 2) * DIFF_VDIM)

    rope_p = _rope_tables(jnp.arange(S, dtype=jnp.int32))
    rope_s = _rope_tables(jnp.full((NB_S,), past, dtype=jnp.int32))
    assert page == LANES
    n_pool = cache_sb_kv.shape[1]
    feature_major = lambda c: jnp.transpose(c, (0, 1, 3, 4, 5, 2))
    cache_sb_t = feature_major(cache_sb_kv).reshape(depth, n_pool, -1, page)
    cache_nsa_t = feature_major(cache_nsa_kv).reshape(depth, n_pool, -1, page)
    cache_diff_r = cache_diff_kv.reshape(depth, n_pool, -1, DIFF_VDIM)
    wlen = state_nsa_win.shape[2]
    win_state_t = feature_major(state_nsa_win).reshape(depth, NB_S, -1, wlen)

    xp = x_prompt.reshape(T, D)
    xs = x_sample.reshape(NB_S, D)
    yp = ys = None
    outs = {k: [] for k in ("sb_p", "sb_s", "nsa_p", "nsa_s", "diff_p", "diff_s", "win_p", "win_s", "conv_p", "conv_s")}
    for l in range(depth):
        lam_init = 0.8 - 0.6 * math.exp(-0.3 * l)
        consts = jnp.full((HALO, LANES), lam_init, F32)
        g_attn = ln_attn[l].reshape(1, D)
        g_ffn = ln_ffn[l].reshape(1, D)
        g_fin = ln_final.reshape(1, D)
        subln = diff_subln[l].reshape(1, DIFF_VDIM)
        pe2, wk2, wv2 = _compress_weights(nsa_cmp_pe[l], nsa_cmp_wk[l], nsa_cmp_wv[l])
        conv_b = ffn_conv_b[l].reshape(1, dff)

        (sbq, sbkv, sbkv_b, nsaq, nsakv, nsakv_b, winkv, winkv_b, nsag, dq, dkv, dkv_b, mg) = _inproj(
            xp, g_attn, w_in_b[l], rope_p, tm, S // tm)
        o_sb = _sb_prompt(sbq, sbkv_b, B, S, tq)
        o_diff = _diff_prompt(dq, dkv_b, diff_lambda[l], subln, consts, B, S, tq)
        cb = _compress_prompt(nsakv, pe2, wk2, wv2, B, S)
        o_nsa = _nsa_prompt(nsaq, nsag, cb, nsakv_b, winkv_b, B, S, tq_nsa)
        xp = _merge(xp, mg, o_sb, o_nsa, o_diff, w_sb_b[l], w_nsa_b[l], w_diff_b[l], w_o_b[l], tm)
        xp, tails, yp = _ffn_prompt(xp, g_ffn, w_up_b[l], ffn_conv_w[l], conv_b, w_down_b[l], g_fin, S, tm)
        outs["sb_p"].append(sbkv.reshape(B, S, 2, SB_KV, HEAD_DIM))
        outs["nsa_p"].append(nsakv.reshape(B, S, 4, NSA_KV, HEAD_DIM))
        outs["diff_p"].append(dkv.reshape(B, S, 2, DIFF_KV, DIFF_VDIM))
        keep = min(NSA_WINDOW, S)
        outs["win_p"].append(winkv.reshape(B, S, 2, NSA_KV, HEAD_DIM)[:, S - keep:])
        outs["conv_p"].append(tails.reshape(B, S // tm, HALO, dff)[:, -1, HALO - (CONV_W - 1):])

        (sbq, sbkv, _, nsaq, nsakv, _, winkv, _, nsag, dq, dkv, _, mg) = _inproj(
            xs, g_attn, w_in_b[l], rope_s, NB_S, 1)
        o_sb = _sb_decode(cache_sb_t, l, page_table, sbq.reshape(NB_S, SB_HEADS, LANES))
        dq4 = dq.reshape(NB_S, DIFF_HEADS, LANES).astype(F32)
        lane = jnp.arange(LANES) < HEAD_DIM
        dq8 = jnp.concatenate([jnp.where(lane, dq4, 0.0), jnp.where(lane, 0.0, dq4)], axis=1).astype(BF16)
        o_diff = _diff_decode(cache_diff_r, l, page_table, dq8, dkv.reshape(NB_S, 1, -1),
                              diff_lambda[l], subln, consts)
        cb = _compress_decode(cache_nsa_t, l, page_table, pe2, wk2, wv2, min(8, NB_S))
        o_nsa = _nsa_decode(cache_nsa_t, l, page_table, nsaq.reshape(NB_S, NSA_HEADS, LANES),
                            nsag.reshape(NB_S, 1, LANES), cb.reshape(NB_S, past // NSA_BLOCK, 2 * LANES),
                            nsakv.reshape(NB_S, 1, -1), win_state_t, winkv.reshape(NB_S, 1, -1))
        xs = _merge(xs, mg, o_sb.reshape(NB_S, -1), o_nsa.reshape(NB_S, -1), o_diff.reshape(NB_S, -1),
                    w_sb_s[l], w_nsa_s[l], w_diff_s[l], w_o_b[l], NB_S)
        prev = state_ffn_conv[l]
        xs, a_new, ys = _ffn_sample(xs, prev[:, 0], prev[:, 1], g_ffn, w_up_b[l], ffn_conv_w[l], conv_b,
                                    w_down_b[l], g_fin)
        outs["sb_s"].append(sbkv.reshape(NB_S, 1, 2, SB_KV, HEAD_DIM))
        outs["nsa_s"].append(nsakv.reshape(NB_S, 1, 4, NSA_KV, HEAD_DIM))
        outs["diff_s"].append(dkv.reshape(NB_S, 1, 2, DIFF_KV, DIFF_VDIM))
        win_full = jnp.concatenate([state_nsa_win[l], winkv.reshape(NB_S, 1, 2, NSA_KV, HEAD_DIM)], axis=1)
        outs["win_s"].append(win_full[:, win_full.shape[1] - wlen:])
        outs["conv_s"].append(jnp.concatenate([prev[:, 1:], a_new[:, None]], axis=1))

    st = lambda k: jnp.stack(outs[k])
    return (yp.reshape(B, S, D), ys.reshape(NB_S, 1, D),
            st("sb_p"), st("sb_s"), st("nsa_p"), st("nsa_s"), st("diff_p"), st("diff_s"),
            st("win_p"), st("win_s"), st("conv_p"), st("conv_s"))
```

```python
import functools
import math

import jax
import jax.numpy as jnp
from jax import lax
from jax.experimental import pallas as pl
from jax.experimental.pallas import tpu as pltpu

F32 = jnp.float32
BF16 = jnp.bfloat16

HEAD_DIM = 64
SB_HEADS, SB_KV = 8, 4
NSA_HEADS, NSA_KV = 8, 2
NSA_REP = NSA_HEADS // NSA_KV
NSA_BLOCK = 64
NSA_SEL = 16
NSA_WINDOW = 512
DIFF_HEADS, DIFF_KV = 4, 2
DIFF_VDIM = 2 * HEAD_DIM
ROPE_THETA = 500000.0
ROT_DIM = HEAD_DIM // 4
CONV_W = 3
EPS = 1e-6
NEG_INF = -1e30
F32_EXP_UNDERFLOW = -104.0
SCALE = HEAD_DIM ** -0.5

LANES = 128
HALO = 8

C_SBQ = (0, 512)
C_SBKV = (512, 1024)
C_NSAQ = (1024, 1536)
C_NSAKV = (1536, 2048)
C_WIN = (2048, 2304)
C_NSAG = (2304, 2432)
C_DQ = (2432, 2944)
C_DKV = (2944, 3456)
C_MG = 3456
N_GATE_RAW = 3 * NSA_HEADS

VMEM_LIMIT = 56 * 1024 * 1024


def _cparams(sem):
    return pltpu.CompilerParams(dimension_semantics=sem, vmem_limit_bytes=VMEM_LIMIT)


def _dot(a, b):
    return jnp.dot(a, b, preferred_element_type=F32)


def _dot_nt(a, b):
    return lax.dot_general(a, b, (((1,), (1,)), ((), ())), preferred_element_type=F32)


def _rms(x, g):
    ms = jnp.mean(x * x, axis=-1, keepdims=True)
    return x * lax.rsqrt(ms + EPS) * g


def _sigmoid(x):
    return 1.0 / (1.0 + jnp.exp(-x))


def _log_keep(z):
    return -(jnp.maximum(z, 0.0) + jnp.log(1.0 + jnp.exp(-jnp.abs(z))))


def _resident(shape, index_map):
    return pl.BlockSpec(shape, index_map, pipeline_mode=pl.Buffered(1))


def _inproj_kernel(x_ref, g_ref, w_ref, cos_ref, s1_ref, s2_ref,
                   sbq_ref, sbkv_ref, sbkvb_ref, nsaq_ref, nsakv_ref, nsakvb_ref,
                   winkv_ref, winkvb_ref, nsag_ref, dq_ref, dkv_ref, dkvb_ref, mg_ref):
    tm = x_ref.shape[0]
    hb = _rms(x_ref[...], g_ref[...]).astype(BF16)
    cos, s1, s2 = cos_ref[...], s1_ref[...], s2_ref[...]
    lo_half = lax.broadcasted_iota(jnp.int32, (tm, LANES), 1) < HEAD_DIM

    def mm(c):
        return _dot(hb, w_ref[:, c[0]:c[1]])

    def rope(u):
        return u * cos + pltpu.roll(u, LANES - ROT_DIM // 2, 1) * s1 + pltpu.roll(u, ROT_DIM // 2, 1) * s2

    def tiles(u):
        return [u[:, t * LANES:(t + 1) * LANES] for t in range(u.shape[1] // LANES)]

    def padded_heads(u, half_of_head, do_rope):
        out = []
        for t, ut in enumerate(tiles(u)):
            if do_rope:
                ut = rope(ut)
            ut = ut * SCALE
            ur = pltpu.roll(ut, HEAD_DIM, 1)
            for hh in range(2):
                tgt = half_of_head(2 * t + hh)
                src = ut if tgt == hh else ur
                out.append(jnp.where(lo_half if tgt == 0 else jnp.logical_not(lo_half), src, 0.0))
        return jnp.concatenate(out, axis=1).astype(BF16)

    sbq_ref[...] = padded_heads(mm(C_SBQ), lambda h: (h // 2) % 2, False)
    u = mm(C_SBKV)
    sbkv_ref[...] = u
    sbkvb_ref[...] = u.astype(BF16)
    nsaq_ref[...] = padded_heads(mm(C_NSAQ), lambda h: h // NSA_REP, True)
    t = tiles(mm(C_NSAKV))
    u = jnp.concatenate([rope(t[0]), t[1], rope(t[2]), t[3]], axis=1)
    nsakv_ref[...] = u
    nsakvb_ref[...] = u.astype(BF16)
    t = tiles(mm(C_WIN))
    u = jnp.concatenate([rope(t[0]), t[1]], axis=1)
    winkv_ref[...] = u
    winkvb_ref[...] = u.astype(BF16)
    nsag_ref[...] = _sigmoid(mm(C_NSAG))
    dq_ref[...] = (jnp.concatenate([rope(ut) for ut in tiles(mm(C_DQ))], axis=1) * SCALE).astype(BF16)
    t = tiles(mm(C_DKV))
    u = jnp.concatenate([rope(t[0]), rope(t[1]), t[2], t[3]], axis=1)
    dkv_ref[...] = u
    dkvb_ref[...] = u.astype(BF16)
    d3 = mg_ref.shape[1]
    for c in range(3):
        mg_ref[:, c * (d3 // 3):(c + 1) * (d3 // 3)] = _sigmoid(
            mm((C_MG + c * (d3 // 3), C_MG + (c + 1) * (d3 // 3))))


def _inproj(x2d, g, w, rope_tabs, tm, n_pos_tiles):
    T, D = x2d.shape
    n_w = w.shape[1]
    row = lambda i: (i, 0)
    tab = lambda i: (i % n_pos_tiles, 0)
    widths = [(1024, BF16), (512, F32), (512, BF16), (1024, BF16), (512, F32), (512, BF16),
              (256, F32), (256, BF16), (LANES, F32), (512, BF16), (512, F32), (512, BF16), (3 * D, F32)]
    return pl.pallas_call(
        _inproj_kernel,
        grid=(T // tm,),
        in_specs=[pl.BlockSpec((tm, D), row), _resident((1, D), lambda i: (0, 0)),
                  _resident((D, n_w), lambda i: (0, 0)),
                  pl.BlockSpec((tm, LANES), tab), pl.BlockSpec((tm, LANES), tab), pl.BlockSpec((tm, LANES), tab)],
        out_specs=[pl.BlockSpec((tm, wd), row) for wd, _ in widths],
        out_shape=[jax.ShapeDtypeStruct((T, wd), dt) for wd, dt in widths],
        compiler_params=_cparams(("parallel",)),
        name="inproj",
    )(x2d, g, w, *rope_tabs)


def _rope_tables(pos):
    half = ROT_DIM // 2
    inv_freq = ROPE_THETA ** (-jnp.arange(half, dtype=F32) * 2.0 / ROT_DIM)
    ang = pos.astype(F32)[:, None] * inv_freq
    cos, sin = jnp.cos(ang), jnp.sin(ang)
    lane = jnp.arange(LANES) % HEAD_DIM
    idx = lane % half
    c = jnp.where(lane < ROT_DIM, cos[:, idx], 1.0)
    s1 = jnp.where(lane < half, -sin[:, idx], 0.0)
    s2 = jnp.where((lane >= half) & (lane < ROT_DIM), sin[:, idx], 0.0)
    return c, s1, s2


def _sb_kernel(q_ref, k_ref, v_ref, o_ref):
    tq = q_ref.shape[0]
    tk = tq
    i = pl.program_id(2)
    n_heads = q_ref.shape[1] // LANES
    rows = n_heads * tq
    q2 = jnp.concatenate([q_ref[:, h * LANES:(h + 1) * LANES] for h in range(n_heads)], axis=0)
    tri = (lax.broadcasted_iota(jnp.int32, (tk, tk), 0) > lax.broadcasted_iota(jnp.int32, (tk, tk), 1)).astype(BF16)
    tri2 = jnp.concatenate([tri, tri], axis=0)

    def block(j, carry, diagonal):
        acc, run = carry
        k = k_ref[pl.ds(pl.multiple_of(j * tk, tk), tk), :]
        v = v_ref[pl.ds(pl.multiple_of(j * tk, tk), tk), :]
        z = _dot_nt(q2, k)
        lk = _log_keep(z)
        if diagonal:
            qrow = lax.broadcasted_iota(jnp.int32, (rows, tk), 0) % tq
            mask = lax.broadcasted_iota(jnp.int32, (rows, tk), 1) < qrow
            lk = jnp.where(mask, lk, 0.0)
        hi = lk.astype(BF16)
        lo = (lk - hi.astype(F32)).astype(BF16)
        between = _dot(jnp.concatenate([hi, lo], axis=1), tri2)
        e = jnp.exp(lk + z + between + run)
        if diagonal:
            e = jnp.where(mask, e, 0.0)
        acc = acc + _dot(e.astype(BF16), v)
        run = run + jnp.sum(lk, axis=1, keepdims=True)
        return acc, run

    acc, run = block(i, (jnp.zeros((rows, LANES), F32), jnp.zeros((rows, 1), F32)), True)

    def more(state):
        t, _, run = state
        return jnp.logical_and(t < i, jnp.max(run) > F32_EXP_UNDERFLOW)

    def step(state):
        t, acc, run = state
        acc, run = block(i - 1 - t, (acc, run), False)
        return t + 1, acc, run

    _, acc, _ = lax.while_loop(more, step, (jnp.int32(0), acc, run))
    a = [acc[h * tq:(h + 1) * tq] for h in range(n_heads)]
    lo_half = lax.broadcasted_iota(jnp.int32, (tq, LANES), 1) < HEAD_DIM
    first = jnp.where(lo_half, a[0], pltpu.roll(a[1], HEAD_DIM, 1))
    second = jnp.where(lo_half, pltpu.roll(a[2], HEAD_DIM, 1), a[3])
    o_ref[...] = jnp.concatenate([first, second], axis=1).astype(o_ref.dtype)


def _sb_prompt(q_pad, kv_b, B, S, tq):
    T = B * S
    nq = S // tq
    n_pairs = SB_KV // 2
    return pl.pallas_call(
        _sb_kernel,
        grid=(B, n_pairs, nq),
        in_specs=[pl.BlockSpec((tq, 4 * LANES), lambda b, p, i: (b * nq + i, p)),
                  pl.BlockSpec((S, LANES), lambda b, p, i: (b, p)),
                  pl.BlockSpec((S, LANES), lambda b, p, i: (b, n_pairs + p))],
        out_specs=pl.BlockSpec((tq, 2 * LANES), lambda b, p, i: (b * nq + i, p)),
        out_shape=jax.ShapeDtypeStruct((T, SB_HEADS * HEAD_DIM), BF16),
        compiler_params=_cparams(("parallel", "parallel", "arbitrary")),
        name="sb_prompt",
    )(q_pad, kv_b, kv_b)


def _diff_lambda(lamv_ref, c_ref):
    lv = lamv_ref[...]
    a = jnp.sum(lv[0:1] * lv[1:2], axis=1, keepdims=True)
    b = jnp.sum(lv[2:3] * lv[3:4], axis=1, keepdims=True)
    lam_init = c_ref[0:1, 0:1]
    return jnp.exp(a) - jnp.exp(b) + lam_init, lam_init


def _softmax_step(s, mask, carry, v):
    m, l, acc = carry
    if mask is not None:
        s = jnp.where(mask, s, NEG_INF)
    m_new = jnp.maximum(m, jnp.max(s, axis=1, keepdims=True))
    p = jnp.exp(s - m_new)
    if mask is not None:
        p = jnp.where(mask, p, 0.0)
    alpha = jnp.exp(m - m_new)
    l = alpha * l + jnp.sum(p, axis=1, keepdims=True)
    acc = alpha * acc + _dot(p.astype(BF16), v)
    return m_new, l, acc


def _softmax_init(rows, width):
    return (jnp.full((rows, 1), NEG_INF, F32), jnp.zeros((rows, 1), F32), jnp.zeros((rows, width), F32))


def _diff_kernel(q_ref, k_ref, v_ref, lamv_ref, subln_ref, c_ref, o_ref):
    tq = q_ref.shape[0]
    tk = min(4 * tq, k_ref.shape[0])
    i = pl.program_id(2)
    j_diag = (i * tq) // tk
    rows = 4 * tq
    lo_half = lax.broadcasted_iota(jnp.int32, (tq, LANES), 1) < HEAD_DIM
    qs = []
    for m in range(2):
        for h in range(2):
            qh = q_ref[:, h * LANES:(h + 1) * LANES].astype(F32)
            qs.append(jnp.where(lo_half if m == 0 else jnp.logical_not(lo_half), qh, 0.0).astype(BF16))
    q4 = jnp.concatenate(qs, axis=0)

    def block(j, carry, diagonal):
        k = k_ref[pl.ds(pl.multiple_of(j * tk, tk), tk), :]
        v = v_ref[pl.ds(pl.multiple_of(j * tk, tk), tk), :]
        s = _dot_nt(q4, k)
        mask = None
        if diagonal:
            r = lax.broadcasted_iota(jnp.int32, (rows, tk), 0)
            mask = j * tk + lax.broadcasted_iota(jnp.int32, (rows, tk), 1) <= i * tq + (r % tq)
        return _softmax_step(s, mask, carry, v)

    carry = block(j_diag, _softmax_init(rows, LANES), True)
    _, l, acc = lax.fori_loop(0, j_diag, lambda t, c: block(t, c, False), carry)
    a = acc / l
    lam, lam_init = _diff_lambda(lamv_ref, c_ref)
    outs = []
    for h in range(2):
        o = a[h * tq:(h + 1) * tq] - lam * a[(2 + h) * tq:(3 + h) * tq]
        outs.append(_rms(o, subln_ref[...]) * (1.0 - lam_init))
    o_ref[...] = jnp.concatenate(outs, axis=1).astype(o_ref.dtype)


def _diff_prompt(q_b, kv_b, lamv, subln, consts, B, S, tq):
    T = B * S
    nq = S // tq
    whole = lambda b, g, i: (0, 0)
    return pl.pallas_call(
        _diff_kernel,
        grid=(B, DIFF_KV, nq),
        in_specs=[pl.BlockSpec((tq, 2 * LANES), lambda b, g, i: (b * nq + i, g)),
                  pl.BlockSpec((S, LANES), lambda b, g, i: (b, g)),
                  pl.BlockSpec((S, LANES), lambda b, g, i: (b, 2 + g)),
                  pl.BlockSpec(lamv.shape, whole), pl.BlockSpec(subln.shape, whole),
                  pl.BlockSpec(consts.shape, whole)],
        out_specs=pl.BlockSpec((tq, 2 * LANES), lambda b, g, i: (b * nq + i, g)),
        out_shape=jax.ShapeDtypeStruct((T, DIFF_HEADS * DIFF_VDIM), BF16),
        compiler_params=_cparams(("parallel", "parallel", "arbitrary")),
        name="diff_prompt",
    )(q_b, kv_b, kv_b, lamv, subln, consts)


def _compress_blocks(buf_k, buf_v, pe_ref, wk_ref, wv_ref, o_ref):
    nblk = buf_k.shape[0] // NSA_BLOCK

    def body(j, acc):
        pe = pe_ref[pl.ds(j, 1), :]
        xk = (buf_k[pl.ds(j, nblk, stride=NSA_BLOCK), :] + pe).astype(BF16)
        xv = (buf_v[pl.ds(j, nblk, stride=NSA_BLOCK), :] + pe).astype(BF16)
        return acc + jnp.concatenate([_dot(xk, wk_ref[j]), _dot(xv, wv_ref[j])], axis=1)

    o_ref[...] = lax.fori_loop(0, NSA_BLOCK, body, jnp.zeros((nblk, 2 * LANES), F32)).astype(o_ref.dtype)


def _compress_prompt_kernel(x_ref, pe_ref, wk_ref, wv_ref, o_ref, buf_k, buf_v):
    buf_k[...] = x_ref[:, :LANES]
    buf_v[...] = x_ref[:, LANES:]
    _compress_blocks(buf_k, buf_v, pe_ref, wk_ref, wv_ref, o_ref)


def _compress_decode_kernel(pt_ref, pe_ref, wk_ref, wv_ref, *refs, seqs_per_block):
    *pages, o_ref, buf_k, buf_v = refs
    page = pages[0].shape[1]
    t = pl.program_id(0) % seqs_per_block
    for k, p_ref in enumerate(pages):
        x = p_ref[...].T
        dst = pl.ds(pl.multiple_of((t * len(pages) + k) * page, page), page)
        buf_k[dst, :] = x[:, :LANES]
        buf_v[dst, :] = x[:, LANES:]

    @pl.when(t == seqs_per_block - 1)
    def _():
        _compress_blocks(buf_k, buf_v, pe_ref, wk_ref, wv_ref, o_ref)


def _compress_weights(pe, wk, wv):
    eye = jnp.eye(NSA_KV, dtype=F32)
    bd = lambda w: jnp.einsum('ab,jde->jadbe', eye, w).reshape(NSA_BLOCK, LANES, LANES).astype(BF16)
    return jnp.tile(pe, (1, NSA_KV)), bd(wk), bd(wv)


def _compress_prompt(nsakv, pe2, wk2, wv2, B, S):
    chunk = min(S, 2048)
    nc = S // chunk
    whole2 = lambda b: (0, 0)
    whole3 = lambda b: (0, 0, 0)
    return pl.pallas_call(
        _compress_prompt_kernel,
        grid=(B * nc,),
        in_specs=[pl.BlockSpec((chunk, 2 * LANES), lambda b: (b, 0)),
                  pl.BlockSpec(pe2.shape, whole2), pl.BlockSpec(wk2.shape, whole3), pl.BlockSpec(wv2.shape, whole3)],
        out_specs=pl.BlockSpec((chunk // NSA_BLOCK, 2 * LANES), lambda b: (b, 0)),
        out_shape=jax.ShapeDtypeStruct((B * S // NSA_BLOCK, 2 * LANES), BF16),
        scratch_shapes=[pltpu.VMEM((chunk, LANES), F32)] * 2,
        compiler_params=_cparams(("parallel",)),
        name="compress_prompt",
    )(nsakv, pe2, wk2, wv2)


def _page_specs(npg, layer, rows, row_block):
    def spec(k):
        return pl.BlockSpec((None, None, rows, LANES), lambda b, pt: (layer, pt[b, k], row_block, 0))
    return [spec(k) for k in range(npg)]


def _compress_decode(cache_t, layer, page_table, pe2, wk2, wv2, seqs_per_block):
    nb, npg = page_table.shape
    page = cache_t.shape[3]
    sg = seqs_per_block
    rows = sg * npg * page
    whole2 = lambda b, pt: (0, 0)
    whole3 = lambda b, pt: (0, 0, 0)
    grid_spec = pltpu.PrefetchScalarGridSpec(
        num_scalar_prefetch=1,
        grid=(nb,),
        in_specs=[pl.BlockSpec(pe2.shape, whole2), pl.BlockSpec(wk2.shape, whole3), pl.BlockSpec(wv2.shape, whole3)]
        + _page_specs(npg, layer, 2 * LANES, 0),
        out_specs=pl.BlockSpec((rows // NSA_BLOCK, 2 * LANES), lambda b, pt: (b // sg, 0)),
        scratch_shapes=[pltpu.VMEM((rows, LANES), F32)] * 2,
    )
    return pl.pallas_call(
        functools.partial(_compress_decode_kernel, seqs_per_block=sg),
        grid_spec=grid_spec,
        out_shape=jax.ShapeDtypeStruct((nb * npg * page // NSA_BLOCK, 2 * LANES), BF16),
        compiler_params=_cparams(("arbitrary",)),
        name="compress_decode",
    )(page_table, pe2, wk2, wv2, *([cache_t] * npg))


def _top_blocks(imp, n_top, n_cand):
    col = lax.broadcasted_iota(jnp.int32, imp.shape, 1)

    def body(t, carry):
        vals, sel = carry
        hit = col == jnp.argmax(vals, axis=1, keepdims=True)
        sel = jnp.where(jnp.logical_and(hit, t < n_cand), 1.0, sel)
        vals = jnp.where(hit, -2.0, vals)
        return vals, sel

    _, sel = lax.fori_loop(0, n_top, body, (imp, jnp.zeros(imp.shape, F32)))
    return sel


def _compressed_branch(q, kc, vc, n_valid_of_row, cur_of_row, group_rows):
    sc = _dot_nt(q, kc)
    col = lax.broadcasted_iota(jnp.int32, sc.shape, 1)
    cmask = col < n_valid_of_row
    mx = jnp.max(jnp.where(cmask, sc, NEG_INF), axis=1, keepdims=True)
    e = jnp.where(cmask, jnp.exp(sc - mx), 0.0)
    l = jnp.sum(e, axis=1, keepdims=True)
    pc = e / jnp.where(l > 0.0, l, 1.0)
    return _dot(pc.astype(BF16), vc), pc


def _gate(gates, head, branch, rows):
    c = head * 3 + branch
    return gates[:, c:c + 1]


def _nsa_kernel(q_ref, gate_ref, cb_ref, sel_ref, win_ref, onehot_ref, o_ref, *, n_top):
    tq = q_ref.shape[0]
    nb = cb_ref.shape[0]
    tk = min(8 * LANES, sel_ref.shape[0])
    tkw = min(2 * LANES, sel_ref.shape[0])
    i = pl.program_id(1)
    q0 = i * tq
    j_diag = q0 // tk
    rows = NSA_HEADS * tq
    pos1 = q0 + lax.broadcasted_iota(jnp.int32, (tq, 1), 0)
    pos8 = q0 + lax.broadcasted_iota(jnp.int32, (rows, 1), 0) % tq
    kc, vc = cb_ref[:, :LANES], cb_ref[:, LANES:]
    q8 = jnp.concatenate([q_ref[:, h * LANES:(h + 1) * LANES] for h in range(NSA_HEADS)], axis=0)
    o_cmp, pc = _compressed_branch(q8, kc, vc, (pos8 + 1) // NSA_BLOCK, None, None)
    imps = []
    for g in range(NSA_KV):
        imp = pc[NSA_REP * g * tq:(NSA_REP * g + 1) * tq]
        for r in range(1, NSA_REP):
            imp = imp + pc[(NSA_REP * g + r) * tq:(NSA_REP * g + r + 1) * tq]
        imps.append(imp)
    cur2 = jnp.concatenate([pos1, pos1], axis=0) // NSA_BLOCK
    col2 = lax.broadcasted_iota(jnp.int32, (2 * tq, nb), 1)
    imp2 = jnp.where(col2 < cur2, jnp.concatenate(imps, axis=0), -1.0)
    sel2 = jnp.where(col2 == cur2, 1.0, _top_blocks(imp2, n_top, cur2))

    gates = gate_ref[...]
    lo_half = lax.broadcasted_iota(jnp.int32, (tq, LANES), 1) < HEAD_DIM
    n_win_blocks = (q0 + tq - 1) // tkw + 1
    w_lo = jnp.maximum(q0 - NSA_WINDOW, 0) // tkw
    m8 = jnp.concatenate([sel2[g * tq:(g + 1) * tq] for g in range(NSA_KV) for _ in range(NSA_REP)], axis=0)
    qx = jnp.concatenate([q8, ((m8 - 1.0) * -NEG_INF).astype(BF16)], axis=1)

    def sel_scores(j):
        k0 = pl.multiple_of(j * tk, tk)
        kx = jnp.concatenate([sel_ref[pl.ds(k0, tk), :LANES], onehot_ref[pl.ds(k0, tk), :]], axis=1)
        return _dot_nt(qx, kx), sel_ref[pl.ds(k0, tk), LANES:]

    s, v = sel_scores(j_diag)
    kpos = j_diag * tk + lax.broadcasted_iota(jnp.int32, (rows, tk), 1)
    carry = _softmax_step(jnp.where(kpos <= pos8, s, NEG_INF), None, _softmax_init(rows, LANES), v)

    def sel_block(j, carry):
        s, v = sel_scores(j)
        return _softmax_step(s, None, carry, v)

    _, l, acc = lax.fori_loop(0, j_diag, sel_block, carry)
    o_sel = acc / l

    def win_block(j, carry):
        k0 = pl.multiple_of(j * tkw, tkw)
        k = win_ref[pl.ds(k0, tkw), :LANES]
        v = win_ref[pl.ds(k0, tkw), LANES:]
        s = _dot_nt(q8, k)
        kpos = k0 + lax.broadcasted_iota(jnp.int32, (rows, tkw), 1)
        return _softmax_step(s, jnp.logical_and(kpos <= pos8, kpos >= pos8 - NSA_WINDOW), carry, v)

    _, l, acc = lax.fori_loop(w_lo, n_win_blocks, win_block, _softmax_init(rows, LANES))
    o_win = acc / l

    heads = []
    for h in range(NSA_HEADS):
        sl = slice(h * tq, (h + 1) * tq)
        heads.append(_gate(gates, h, 0, tq) * o_cmp[sl] + _gate(gates, h, 1, tq) * o_sel[sl]
                     + _gate(gates, h, 2, tq) * o_win[sl])
    out_tiles = []
    for t in range(NSA_HEADS // 2):
        a, b = heads[2 * t], heads[2 * t + 1]
        if (2 * t) // NSA_REP == 0:
            out_tiles.append(jnp.where(lo_half, a, pltpu.roll(b, HEAD_DIM, 1)))
        else:
            out_tiles.append(jnp.where(lo_half, pltpu.roll(a, HEAD_DIM, 1), b))
    o_ref[...] = jnp.concatenate(out_tiles, axis=1).astype(o_ref.dtype)


def _nsa_prompt(q_pad, gates, cb, nsakv_b, winkv_b, B, S, tq):
    T = B * S
    nq = S // tq
    nb = S // NSA_BLOCK
    onehot = (jnp.arange(S)[:, None] // NSA_BLOCK == jnp.arange(nb)[None, :]).astype(BF16)
    return pl.pallas_call(
        functools.partial(_nsa_kernel, n_top=min(NSA_SEL - 1, nb)),
        grid=(B, nq),
        in_specs=[pl.BlockSpec((tq, NSA_HEADS * LANES), lambda b, i: (b * nq + i, 0)),
                  pl.BlockSpec((tq, LANES), lambda b, i: (b * nq + i, 0)),
                  pl.BlockSpec((nb, 2 * LANES), lambda b, i: (b, 0)),
                  pl.BlockSpec((S, 2 * LANES), lambda b, i: (b, 1)),
                  pl.BlockSpec((S, 2 * LANES), lambda b, i: (b, 0)),
                  _resident((S, nb), lambda b, i: (0, 0))],
        out_specs=pl.BlockSpec((tq, NSA_HEADS * HEAD_DIM), lambda b, i: (b * nq + i, 0)),
        out_shape=jax.ShapeDtypeStruct((T, NSA_HEADS * HEAD_DIM), BF16),
        compiler_params=_cparams(("parallel", "arbitrary")),
        name="nsa_prompt",
    )(q_pad, gates, cb, nsakv_b, winkv_b, onehot)


def _merge_kernel(x_ref, mg_ref, osb_ref, onsa_ref, odiff_ref, wsb_ref, wnsa_ref, wdiff_ref, wo_ref, y_ref):
    d = x_ref.shape[1]
    merged = (mg_ref[:, 0:d] * _dot(osb_ref[...].astype(BF16), wsb_ref[...])
              + mg_ref[:, d:2 * d] * _dot(onsa_ref[...].astype(BF16), wnsa_ref[...])
              + mg_ref[:, 2 * d:3 * d] * _dot(odiff_ref[...].astype(BF16), wdiff_ref[...]))
    y_ref[...] = x_ref[...] + _dot(merged.astype(BF16), wo_ref[...])


def _merge(x2d, mg, o_sb, o_nsa, o_diff, w_sb, w_nsa, w_diff, w_o, tm):
    T, D = x2d.shape
    row = lambda i: (i, 0)
    whole = lambda i: (0, 0)
    return pl.pallas_call(
        _merge_kernel,
        grid=(T // tm,),
        in_specs=[pl.BlockSpec((tm, D), row), pl.BlockSpec((tm, 3 * D), row),
                  pl.BlockSpec((tm, o_sb.shape[1]), row), pl.BlockSpec((tm, o_nsa.shape[1]), row),
                  pl.BlockSpec((tm, o_diff.shape[1]), row),
                  _resident(w_sb.shape, whole), _resident(w_nsa.shape, whole),
                  _resident(w_diff.shape, whole), _resident(w_o.shape, whole)],
        out_specs=pl.BlockSpec((tm, D), row),
        out_shape=jax.ShapeDtypeStruct((T, D), F32),
        compiler_params=_cparams(("parallel",)),
        name="merge",
    )(x2d, mg, o_sb, o_nsa, o_diff, w_sb, w_nsa, w_diff, w_o)


def _gelu(c):
    return 0.5 * c * (1.0 + jnp.tanh(math.sqrt(2.0 / math.pi) * (c + 0.044715 * (c * c * c))))


def _ffn_prompt_kernel(x_ref, halo_ref, g_ref, wup_ref, cw_ref, cb_ref, wdown_ref, gf_ref,
                       y_ref, tail_ref, yn_ref, a_scr, *, tiles_per_seq):
    tm = x_ref.shape[0]
    dff = cw_ref.shape[1]
    i = pl.program_id(0)
    x = x_ref[...]
    hb = _rms(x, g_ref[...]).astype(BF16)
    a = _dot(hb, wup_ref[:, :dff])
    gate = _dot(hb, wup_ref[:, dff:])
    a_halo = _dot(_rms(halo_ref[...], g_ref[...]).astype(BF16), wup_ref[:, :dff])
    a_scr[0:HALO, :] = jnp.where(i % tiles_per_seq == 0, 0.0, a_halo)
    a_scr[HALO:HALO + tm, :] = a
    c = cb_ref[...] + a * cw_ref[CONV_W - 1:CONV_W, :]
    for j in range(CONV_W - 1):
        off = HALO - (CONV_W - 1) + j
        c = c + a_scr[off:off + tm, :] * cw_ref[j:j + 1, :]
    y = x + _dot((_gelu(c) * gate).astype(BF16), wdown_ref[...])
    y_ref[...] = y
    yn_ref[...] = _rms(y, gf_ref[...])
    tail_ref[...] = a_scr[tm:tm + HALO, :]


def _ffn_prompt(x2d, g, w_up, conv_w, conv_b, w_down, g_final, S, tm):
    T, D = x2d.shape
    dff = conv_w.shape[1]
    whole = lambda i: (0, 0)
    hb = tm // HALO
    return pl.pallas_call(
        functools.partial(_ffn_prompt_kernel, tiles_per_seq=S // tm),
        grid=(T // tm,),
        in_specs=[pl.BlockSpec((tm, D), lambda i: (i, 0)),
                  pl.BlockSpec((HALO, D), lambda i: (jnp.maximum(i * hb - 1, 0), 0)),
                  _resident((1, D), whole), _resident(w_up.shape, whole), _resident(conv_w.shape, whole),
                  _resident(conv_b.shape, whole), _resident(w_down.shape, whole), _resident((1, D), whole)],
        out_specs=[pl.BlockSpec((tm, D), lambda i: (i, 0)),
                   pl.BlockSpec((None, HALO, dff), lambda i: (i, 0, 0)),
                   pl.BlockSpec((tm, D), lambda i: (i, 0))],
        out_shape=[jax.ShapeDtypeStruct((T, D), F32), jax.ShapeDtypeStruct((T // tm, HALO, dff), F32),
                   jax.ShapeDtypeStruct((T, D), F32)],
        scratch_shapes=[pltpu.VMEM((tm + HALO, dff), F32)],
        compiler_params=_cparams(("parallel",)),
        name="ffn_prompt",
    )(x2d, x2d, g, w_up, conv_w, conv_b, w_down, g_final)


def _ffn_sample_kernel(x_ref, p0_ref, p1_ref, g_ref, wup_ref, cw_ref, cb_ref, wdown_ref, gf_ref,
                       y_ref, a_ref, yn_ref):
    dff = cw_ref.shape[1]
    x = x_ref[...]
    hb = _rms(x, g_ref[...]).astype(BF16)
    a = _dot(hb, wup_ref[:, :dff])
    gate = _dot(hb, wup_ref[:, dff:])
    c = cb_ref[...] + p0_ref[...] * cw_ref[0:1, :] + p1_ref[...] * cw_ref[1:2, :] + a * cw_ref[2:3, :]
    y = x + _dot((_gelu(c) * gate).astype(BF16), wdown_ref[...])
    y_ref[...] = y
    a_ref[...] = a
    yn_ref[...] = _rms(y, gf_ref[...])


def _ffn_sample(x2d, prev0, prev1, g, w_up, conv_w, conv_b, w_down, g_final):
    T, D = x2d.shape
    dff = conv_w.shape[1]
    return pl.pallas_call(
        _ffn_sample_kernel,
        out_shape=[jax.ShapeDtypeStruct((T, D), F32), jax.ShapeDtypeStruct((T, dff), F32),
                   jax.ShapeDtypeStruct((T, D), F32)],
        compiler_params=pltpu.CompilerParams(vmem_limit_bytes=VMEM_LIMIT),
        name="ffn_sample",
    )(x2d, prev0, prev1, g, w_up, conv_w, conv_b, w_down, g_final)


def _head_rows(q_ref, split):
    q8 = q_ref[...].astype(F32)
    row = lax.broadcasted_iota(jnp.int32, q8.shape, 0)
    return jnp.concatenate([jnp.where(row < split, q8, 0.0), jnp.where(row >= split, q8, 0.0)], axis=1)


def _lane_cat(pages, r0, r1):
    return jnp.concatenate([p[r0:r1, :] for p in pages], axis=1).astype(BF16)


def _sb_decode_kernel(pt_ref, q_ref, *refs):
    *pages, o_ref = refs
    npg = len(pages)
    half = pages[0].shape[0] // 2
    tk = pages[0].shape[1]
    nh = q_ref.shape[0]
    q = _head_rows(q_ref, SB_HEADS // 2).astype(BF16)
    z = _dot(q, _lane_cat(pages, 0, half))
    zs = jnp.concatenate([z[:, k * tk:(k + 1) * tk] for k in range(npg)], axis=0)
    lk = _log_keep(zs)
    tri = (lax.broadcasted_iota(jnp.int32, (tk, tk), 0) > lax.broadcasted_iota(jnp.int32, (tk, tk), 1)).astype(BF16)
    hi = lk.astype(BF16)
    lo = (lk - hi.astype(F32)).astype(BF16)
    between = _dot(hi, tri) + _dot(lo, tri)
    tot = jnp.sum(lk, axis=1, keepdims=True)
    runs, run = [], jnp.zeros((nh, 1), F32)
    for k in reversed(range(npg)):
        runs.append(run)
        run = run + tot[k * nh:(k + 1) * nh]
    e = jnp.exp(lk + zs + between + jnp.concatenate(runs[::-1], axis=0))
    e = jnp.concatenate([e[k * nh:(k + 1) * nh] for k in range(npg)], axis=1).astype(BF16)
    acc = _dot_nt(e, _lane_cat(pages, half, 2 * half))
    row = lax.broadcasted_iota(jnp.int32, acc.shape, 0)
    lane = lax.broadcasted_iota(jnp.int32, acc.shape, 1)
    o_ref[...] = jnp.where(lane // HEAD_DIM == row // 2, acc, 0.0)


def _sb_decode(cache_t, layer, page_table, q_pad3):
    nb, npg = page_table.shape
    feat = cache_t.shape[2]
    grid_spec = pltpu.PrefetchScalarGridSpec(
        num_scalar_prefetch=1,
        grid=(nb,),
        in_specs=[pl.BlockSpec((None, SB_HEADS, LANES), lambda b, pt: (b, 0, 0))] + _page_specs(npg, layer, feat, 0),
        out_specs=pl.BlockSpec((None, SB_HEADS, feat // 2), lambda b, pt: (b, 0, 0)),
    )
    return pl.pallas_call(
        _sb_decode_kernel,
        grid_spec=grid_spec,
        out_shape=jax.ShapeDtypeStruct((nb, SB_HEADS, feat // 2), F32),
        compiler_params=_cparams(("parallel",)),
        name="sb_decode",
    )(page_table, q_pad3, *([cache_t] * npg))


def _softmax_with_new_key(s, s_new):
    m = jnp.maximum(jnp.max(s, axis=1, keepdims=True), s_new)
    p = jnp.exp(s - m)
    p_new = jnp.exp(s_new - m)
    return p.astype(BF16), p_new.astype(BF16).astype(F32), jnp.sum(p, axis=1, keepdims=True) + p_new


def _diff_decode_kernel(pt_ref, q_ref, new_ref, lamv_ref, subln_ref, c_ref, *refs):
    *pages, o_ref = refs
    page = pages[0].shape[0] // 4
    half = new_ref.shape[1] // 2
    row = lax.broadcasted_iota(jnp.int32, (2 * DIFF_HEADS, LANES), 0)
    q8 = q_ref[...].astype(F32)
    qf = jnp.concatenate([jnp.where((row % DIFF_HEADS) < 2, q8, 0.0), jnp.where((row % DIFF_HEADS) >= 2, q8, 0.0)], axis=1)

    def rows_of(slot):
        return jnp.concatenate(
            [jnp.concatenate([p[pl.ds(2 * slot + g, page, stride=4), :] for g in range(DIFF_KV)], axis=1)
             for p in pages], axis=0).astype(BF16)

    s = _dot_nt(qf.astype(BF16), rows_of(0))
    k_new = new_ref[:, :half].astype(BF16).astype(F32)
    v_new = new_ref[:, half:].astype(BF16).astype(F32)
    p, p_new, l = _softmax_with_new_key(s, jnp.sum(qf * k_new, axis=1, keepdims=True))
    a = (_dot(p, rows_of(1)) + p_new * v_new) / l
    lam, lam_init = _diff_lambda(lamv_ref, c_ref)
    o = a - lam * pltpu.roll(a, DIFF_HEADS, 0)
    r2 = lax.broadcasted_iota(jnp.int32, o.shape, 0)
    lane = lax.broadcasted_iota(jnp.int32, o.shape, 1)
    valid = jnp.logical_and(r2 < DIFF_HEADS, lane // DIFF_VDIM == r2 // 2)
    o = jnp.where(valid, o, 0.0)
    ms = jnp.sum(o * o, axis=1, keepdims=True) * (1.0 / DIFF_VDIM)
    g2 = jnp.concatenate([subln_ref[...], subln_ref[...]], axis=1)
    o_ref[...] = o * lax.rsqrt(ms + EPS) * g2 * (1.0 - lam_init)


def _diff_decode(cache_r, layer, page_table, q8, new_rows3, lamv, subln, consts):
    nb, npg = page_table.shape
    rows, width = cache_r.shape[2], new_rows3.shape[2]
    whole = lambda b, pt: (0, 0)
    grid_spec = pltpu.PrefetchScalarGridSpec(
        num_scalar_prefetch=1,
        grid=(nb,),
        in_specs=[pl.BlockSpec((None, 2 * DIFF_HEADS, LANES), lambda b, pt: (b, 0, 0)),
                  pl.BlockSpec((None, 1, width), lambda b, pt: (b, 0, 0)),
                  pl.BlockSpec(lamv.shape, whole), pl.BlockSpec(subln.shape, whole), pl.BlockSpec(consts.shape, whole)]
        + _page_specs(npg, layer, rows, 0),
        out_specs=pl.BlockSpec((None, 2 * DIFF_HEADS, width // 2), lambda b, pt: (b, 0, 0)),
    )
    return pl.pallas_call(
        _diff_decode_kernel,
        grid_spec=grid_spec,
        out_shape=jax.ShapeDtypeStruct((nb, 2 * DIFF_HEADS, width // 2), F32),
        compiler_params=_cparams(("parallel",)),
        name="diff_decode",
    )(page_table, q8, new_rows3, lamv, subln, consts, *([cache_r] * npg))


def _nsa_decode_kernel(pt_ref, q_ref, gate_ref, cb_ref, new_ref, win_ref, wnew_ref, *refs, n_top, n_cmp):
    *pages, o_ref, wout_ref = refs
    tk = pages[0].shape[1]
    past = len(pages) * tk
    rows = NSA_HEADS

    w = win_ref[...]
    wlen = w.shape[1]
    wn = jnp.broadcast_to(wnew_ref[...], (w.shape[0], w.shape[0]))
    diag = lax.broadcasted_iota(jnp.int32, wn.shape, 0) == lax.broadcasted_iota(jnp.int32, wn.shape, 1)
    new_col = jnp.sum(jnp.where(diag, wn, 0.0), axis=1, keepdims=True)
    last = lax.broadcasted_iota(jnp.int32, w.shape, 1) == wlen - 1
    wout_ref[...] = jnp.where(last, new_col, pltpu.roll(w, wlen - 1, 1))

    q = q_ref[...]
    qf = q.astype(F32)
    row1 = lax.broadcasted_iota(jnp.int32, (rows, 1), 0)
    bf = lambda x: x.astype(BF16).astype(F32)

    pad = jnp.zeros((LANES - n_cmp, 2 * LANES), BF16)
    cb = jnp.concatenate([cb_ref[...], pad], axis=0)
    o_cmp, pc = _compressed_branch(q, cb[:, :LANES], cb[:, LANES:], n_cmp, None, None)
    rowf = lax.broadcasted_iota(jnp.int32, pc.shape, 0)
    imp0 = jnp.sum(jnp.where(rowf < NSA_REP, pc, 0.0), axis=0, keepdims=True)
    imp1 = jnp.sum(jnp.where(rowf >= NSA_REP, pc, 0.0), axis=0, keepdims=True)
    col = lax.broadcasted_iota(jnp.int32, pc.shape, 1)
    imp = jnp.where(col < n_cmp, jnp.where(rowf < NSA_REP, imp0, imp1), -1.0)
    sel = _top_blocks(imp, n_top, n_cmp)

    blk = lax.broadcasted_iota(jnp.int32, (LANES, past), 1) // NSA_BLOCK
    onehot = (lax.broadcasted_iota(jnp.int32, (LANES, past), 0) == blk).astype(BF16)
    qx = jnp.concatenate([q, ((sel - 1.0) * -NEG_INF).astype(BF16)], axis=1)
    s = _dot(qx, jnp.concatenate([_lane_cat(pages, 0, LANES), onehot], axis=0))
    s_new = jnp.sum(qf * bf(new_ref[:, 2 * LANES:3 * LANES]), axis=1, keepdims=True)
    p, p_new, l = _softmax_with_new_key(s, s_new)
    o_sel = (_dot_nt(p, _lane_cat(pages, LANES, 2 * LANES)) + p_new * bf(new_ref[:, 3 * LANES:])) / l

    s = _dot(q, win_ref[:LANES, :].astype(BF16))
    s_new = jnp.sum(qf * bf(wnew_ref[:, :LANES]), axis=1, keepdims=True)
    p, p_new, l = _softmax_with_new_key(s, s_new)
    o_win = (_dot_nt(p, win_ref[LANES:, :].astype(BF16)) + p_new * bf(wnew_ref[:, LANES:])) / l

    gates = jnp.broadcast_to(gate_ref[...], (rows, LANES))
    lane = lax.broadcasted_iota(jnp.int32, (rows, LANES), 1)
    gsel = lambda c: jnp.sum(jnp.where(lane == 3 * row1 + c, gates, 0.0), axis=1, keepdims=True)
    o = gsel(0) * o_cmp + gsel(1) * o_sel + gsel(2) * o_win
    o_ref[...] = jnp.where(lane // HEAD_DIM == row1 // NSA_REP, o, 0.0)


def _nsa_decode(cache_t, layer, page_table, q8, gates3, cb3, new_rows3, win_state_t, wnew3):
    nb, npg = page_table.shape
    n_cmp = cb3.shape[1]
    wlen = win_state_t.shape[3]
    n_top = min(NSA_SEL - 1, n_cmp + 1)
    grid_spec = pltpu.PrefetchScalarGridSpec(
        num_scalar_prefetch=1,
        grid=(nb,),
        in_specs=[pl.BlockSpec((None, NSA_HEADS, LANES), lambda b, pt: (b, 0, 0)),
                  pl.BlockSpec((None, 1, LANES), lambda b, pt: (b, 0, 0)),
                  pl.BlockSpec((None, n_cmp, 2 * LANES), lambda b, pt: (b, 0, 0)),
                  pl.BlockSpec((None, 1, 4 * LANES), lambda b, pt: (b, 0, 0)),
                  pl.BlockSpec((None, None, 2 * LANES, wlen), lambda b, pt: (layer, b, 0, 0)),
                  pl.BlockSpec((None, 1, 2 * LANES), lambda b, pt: (b, 0, 0))]
        + _page_specs(npg, layer, 2 * LANES, 1),
        out_specs=[pl.BlockSpec((None, NSA_HEADS, LANES), lambda b, pt: (b, 0, 0)),
                   pl.BlockSpec((None, 2 * LANES, wlen), lambda b, pt: (b, 0, 0))],
    )
    return pl.pallas_call(
        functools.partial(_nsa_decode_kernel, n_top=n_top, n_cmp=n_cmp),
        grid_spec=grid_spec,
        out_shape=[jax.ShapeDtypeStruct((nb, NSA_HEADS, LANES), F32),
                   jax.ShapeDtypeStruct((nb, 2 * LANES, wlen), F32)],
        compiler_params=_cparams(("parallel",)),
        name="nsa_decode",
    )(page_table, q8, gates3, cb3, new_rows3, win_state_t, wnew3, *([cache_t] * npg))


def _prep_w_in(w_in):
    g0 = C_NSAG[0]
    gate = w_in[..., g0:g0 + N_GATE_RAW]
    gate = jnp.pad(gate, ((0, 0), (0, 0), (0, LANES - N_GATE_RAW)))
    return jnp.concatenate([w_in[..., :g0], gate, w_in[..., g0 + N_GATE_RAW:]], axis=-1).astype(BF16)


def _padded_out_weight(w, n_rows, row_width, head_width, lane_of_head):
    depth, _, d = w.shape
    out = jnp.zeros((depth, n_rows * row_width, d), w.dtype)
    for h in range(w.shape[1] // head_width):
        start = h * row_width + lane_of_head(h)
        out = out.at[:, start:start + head_width].set(w[:, h * head_width:(h + 1) * head_width])
    return out


def kernel(x_prompt, x_sample, cache_sb_kv, cache_nsa_kv, cache_diff_kv, state_nsa_win, state_ffn_conv, page_table, ln_attn, w_in, nsa_cmp_pe, nsa_cmp_wk, nsa_cmp_wv, diff_lambda, diff_subln, w_sb_out, w_nsa_out, w_diff_out, w_o, ln_ffn, w_ff_up, ffn_conv_w, ffn_conv_b, w_ff_down, ln_final):
    B, S, D = x_prompt.shape
    NB_S = x_sample.shape[0]
    depth = w_in.shape[0]
    npg = page_table.shape[1]
    page = cache_sb_kv.shape[2]
    past = npg * page
    dff = ffn_conv_w.shape[2]
    T = B * S
    tq = min(256, S)
    tq_nsa = min(128, S)
    tm = min(256, S)

    w_in_b = _prep_w_in(w_in)
    w_sb_b, w_nsa_b, w_diff_b = w_sb_out.astype(BF16), w_nsa_out.astype(BF16), w_diff_out.astype(BF16)
    w_o_b, w_up_b, w_down_b = w_o.astype(BF16), w_ff_up.astype(BF16), w_ff_down.astype(BF16)
    w_sb_s = _padded_out_weight(w_sb_b, SB_HEADS, 4 * HEAD_DIM, HEAD_DIM, lambda h: (h // 2) * HEAD_DIM)
    w_nsa_s = _padded_out_weight(w_nsa_b, NSA_HEADS, LANES, HEAD_DIM, lambda h: (h // NSA_REP) * HEAD_DIM)
    w_diff_s = _padded_out_weight(w_diff_b, 2 * DIFF_HEADS, 2 * DIFF_VDIM, DIFF_VDIM, lambda h: (h // 2) * DIFF_VDIM)

    rope_p = _rope_tables(jnp.arange(S, dtype=jnp.int32))
    rope_s = _rope_tables(jnp.full((NB_S,), past, dtype=jnp.int32))
    assert page == LANES
    n_pool = cache_sb_kv.shape[1]
    feature_major = lambda c: jnp.transpose(c, (0, 1, 3, 4, 5, 2))
    cache_sb_t = feature_major(cache_sb_kv).reshape(depth, n_pool, -1, page)
    cache_nsa_t = feature_major(cache_nsa_kv).reshape(depth, n_pool, -1, page)
    cache_diff_r = cache_diff_kv.reshape(depth, n_pool, -1, DIFF_VDIM)
    wlen = state_nsa_win.shape[2]
    win_state_t = feature_major(state_nsa_win).reshape(depth, NB_S, -1, wlen)

    xp = x_prompt.reshape(T, D)
    xs = x_sample.reshape(NB_S, D)
    yp = ys = None
    outs = {k: [] for k in ("sb_p", "sb_s", "nsa_p", "nsa_s", "diff_p", "diff_s", "win_p", "win_s", "conv_p", "conv_s")}
    for l in range(depth):
        lam_init = 0.8 - 0.6 * math.exp(-0.3 * l)
        consts = jnp.full((HALO, LANES), lam_init, F32)
        g_attn = ln_attn[l].reshape(1, D)
        g_ffn = ln_ffn[l].reshape(1, D)
        g_fin = ln_final.reshape(1, D)
        subln = diff_subln[l].reshape(1, DIFF_VDIM)
        pe2, wk2, wv2 = _compress_weights(nsa_cmp_pe[l], nsa_cmp_wk[l], nsa_cmp_wv[l])
        conv_b = ffn_conv_b[l].reshape(1, dff)

        (sbq, sbkv, sbkv_b, nsaq, nsakv, nsakv_b, winkv, winkv_b, nsag, dq, dkv, dkv_b, mg) = _inproj(
            xp, g_attn, w_in_b[l], rope_p, tm, S // tm)
        o_sb = _sb_prompt(sbq, sbkv_b, B, S, tq)
        o_diff = _diff_prompt(dq, dkv_b, diff_lambda[l], subln, consts, B, S, tq)
        cb = _compress_prompt(nsakv, pe2, wk2, wv2, B, S)
        o_nsa = _nsa_prompt(nsaq, nsag, cb, nsakv_b, winkv_b, B, S, tq_nsa)
        xp = _merge(xp, mg, o_sb, o_nsa, o_diff, w_sb_b[l], w_nsa_b[l], w_diff_b[l], w_o_b[l], tm)
        xp, tails, yp = _ffn_prompt(xp, g_ffn, w_up_b[l], ffn_conv_w[l], conv_b, w_down_b[l], g_fin, S, tm)
        outs["sb_p"].append(sbkv.reshape(B, S, 2, SB_KV, HEAD_DIM))
        outs["nsa_p"].append(nsakv.reshape(B, S, 4, NSA_KV, HEAD_DIM))
        outs["diff_p"].append(dkv.reshape(B, S, 2, DIFF_KV, DIFF_VDIM))
        keep = min(NSA_WINDOW, S)
        outs["win_p"].append(winkv.reshape(B, S, 2, NSA_KV, HEAD_DIM)[:, S - keep:])
        outs["conv_p"].append(tails.reshape(B, S // tm, HALO, dff)[:, -1, HALO - (CONV_W - 1):])

        (sbq, sbkv, _, nsaq, nsakv, _, winkv, _, nsag, dq, dkv, _, mg) = _inproj(
            xs, g_attn, w_in_b[l], rope_s, NB_S, 1)
        o_sb = _sb_decode(cache_sb_t, l, page_table, sbq.reshape(NB_S, SB_HEADS, LANES))
        dq4 = dq.reshape(NB_S, DIFF_HEADS, LANES).astype(F32)
        lane = jnp.arange(LANES) < HEAD_DIM
        dq8 = jnp.concatenate([jnp.where(lane, dq4, 0.0), jnp.where(lane, 0.0, dq4)], axis=1).astype(BF16)
        o_diff = _diff_decode(cache_diff_r, l, page_table, dq8, dkv.reshape(NB_S, 1, -1),
                              diff_lambda[l], subln, consts)
        cb = _compress_decode(cache_nsa_t, l, page_table, pe2, wk2, wv2, min(8, NB_S))
        o_nsa, win_next = _nsa_decode(cache_nsa_t, l, page_table, nsaq.reshape(NB_S, NSA_HEADS, LANES),
                            nsag.reshape(NB_S, 1, LANES), cb.reshape(NB_S, past // NSA_BLOCK, 2 * LANES),
                            nsakv.reshape(NB_S, 1, -1), win_state_t, winkv.reshape(NB_S, 1, -1))
        xs = _merge(xs, mg, o_sb.reshape(NB_S, -1), o_nsa.reshape(NB_S, -1), o_diff.reshape(NB_S, -1),
                    w_sb_s[l], w_nsa_s[l], w_diff_s[l], w_o_b[l], NB_S)
        prev = state_ffn_conv[l]
        xs, a_new, ys = _ffn_sample(xs, prev[:, 0], prev[:, 1], g_ffn, w_up_b[l], ffn_conv_w[l], conv_b,
                                    w_down_b[l], g_fin)
        outs["sb_s"].append(sbkv.reshape(NB_S, 1, 2, SB_KV, HEAD_DIM))
        outs["nsa_s"].append(nsakv.reshape(NB_S, 1, 4, NSA_KV, HEAD_DIM))
        outs["diff_s"].append(dkv.reshape(NB_S, 1, 2, DIFF_KV, DIFF_VDIM))
        outs["win_s"].append(jnp.transpose(win_next.reshape(NB_S, 2, NSA_KV, HEAD_DIM, wlen), (0, 4, 1, 2, 3)))
        outs["conv_s"].append(jnp.concatenate([prev[:, 1:], a_new[:, None]], axis=1))

    st = lambda k: jnp.stack(outs[k])
    return (yp.reshape(B, S, D), ys.reshape(NB_S, 1, D),
            st("sb_p"), st("sb_s"), st("nsa_p"), st("nsa_s"), st("diff_p"), st("diff_s"),
            st("win_p"), st("win_s"), st("conv_p"), st("conv_s"))
```
